```python
import jax
import jax.numpy as jnp
from jax import lax
import numpy as np

D_MODEL = 1024
BATCH = 4
SEQ = 4096
DEPTH = 4

D_MIX = D_MODEL
CONV_CH = D_MIX // 4
CONV_K = 31
CONV_IN = 2 * CONV_CH
MLA_NOPE = 64
MLA_ROPE = 32
MLA_V = 64
MLA_QK = MLA_NOPE + MLA_ROPE
MLA_OUT = D_MIX // 2
MLA_HEADS = MLA_OUT // MLA_V
MLA_Q_LORA = D_MODEL // 4
MLA_KV_LORA = D_MODEL // 8
MLA_IN = MLA_Q_LORA + MLA_KV_LORA + MLA_ROPE
ROPE_THETA = 10000.0
Q_BLOCK = 128
RWKV_HEAD = 64
RWKV_CH = D_MIX - CONV_CH - MLA_OUT
RWKV_HEADS = RWKV_CH // RWKV_HEAD
DECAY_LORA = 64
ICLR_LORA = 64
GATE_LORA = 128
RWKV_IN = 3 * RWKV_CH + DECAY_LORA + ICLR_LORA + GATE_LORA
IN_COLS = CONV_IN + MLA_IN + RWKV_IN
D_FF = ((8 * D_MODEL // 3 + 127) // 128) * 128
N_EXPERTS = 8
TOP_K = 2
MOE_FF = 7 * D_MODEL // 2
MOE_BLOCK = 256
N_DENSE = (DEPTH + 1) // 2
N_MOE = DEPTH // 2
N_MOD = 6
RMS_EPS = 1e-6
LN_EPS = 1e-5
RWKV_GN_EPS = 64e-5

kernel_name = 'hybrid_parallel_group_block'


def _rms_norm(x, gain):
    xf = x.astype(jnp.float32)
    y = xf * lax.rsqrt(jnp.mean(xf * xf, axis=-1, keepdims=True) + RMS_EPS)
    return (y * gain.astype(jnp.float32)).astype(x.dtype)


def _layer_norm(x, gain, bias, eps):
    xf = x.astype(jnp.float32)
    xc = xf - jnp.mean(xf, axis=-1, keepdims=True)
    var = jnp.mean(xc * xc, axis=-1, keepdims=True)
    y = xc * lax.rsqrt(var + eps) * gain.astype(jnp.float32) + bias.astype(jnp.float32)
    return y.astype(x.dtype)


def _rope(x, cos, sin):
    xf = x.astype(jnp.float32)
    x1, x2 = jnp.split(xf, 2, axis=-1)
    y = jnp.concatenate([x1 * cos - x2 * sin, x1 * sin + x2 * cos], axis=-1)
    return y.astype(x.dtype)


def _conformer_conv(u, conv_w, conv_b, ln_g, ln_b):
    a, g = jnp.split(u, 2, axis=-1)
    z = a * jax.nn.sigmoid(g)
    z = lax.conv_general_dilated(z, conv_w[:, None, :].astype(z.dtype), window_strides=(1,), padding=[(CONV_K - 1, 0)], dimension_numbers=('NWC', 'WIO', 'NWC'), feature_group_count=CONV_CH)
    z = z + conv_b
    return jax.nn.silu(_layer_norm(z, ln_g, ln_b, LN_EPS))


def _causal_attention(q, k, v):
    b, t, h, dq = q.shape
    nb = t // Q_BLOCK
    scale = float(dq) ** -0.5
    q_blocks = q.reshape(b, nb, Q_BLOCK, h, dq).swapaxes(0, 1)
    k_pos = jnp.arange(t)

    def block(args):
        q_blk, start = args
        s = jnp.einsum('bqhd,bkhd->bhqk', q_blk, k, preferred_element_type=jnp.float32) * scale
        q_pos = start + jnp.arange(Q_BLOCK)
        s = jnp.where(k_pos[None, :] <= q_pos[:, None], s, -jnp.inf)
        p = jax.nn.softmax(s, axis=-1).astype(v.dtype)
        return jnp.einsum('bhqk,bkhd->bqhd', p, v)

    o = lax.map(block, (q_blocks, jnp.arange(nb) * Q_BLOCK))
    return o.swapaxes(0, 1).reshape(b, t, h, v.shape[-1])


def _mla(u, cos, sin, q_norm_g, w_uq, kv_norm_g, w_ukv, qk_q_g, qk_k_g):
    b, t, _ = u.shape
    c_q, c_kv, k_rope = jnp.split(u, [MLA_Q_LORA, MLA_Q_LORA + MLA_KV_LORA], axis=-1)
    q = (_rms_norm(c_q, q_norm_g) @ w_uq).reshape(b, t, MLA_HEADS, MLA_QK)
    kv = (_rms_norm(c_kv, kv_norm_g) @ w_ukv).reshape(b, t, MLA_HEADS, MLA_NOPE + MLA_V)
    k_nope, v = jnp.split(kv, [MLA_NOPE], axis=-1)
    k_rope = jnp.broadcast_to(k_rope[:, :, None, :], (b, t, MLA_HEADS, MLA_ROPE))
    k = jnp.concatenate([k_nope, k_rope], axis=-1)
    q = _rms_norm(q, qk_q_g)
    k = _rms_norm(k, qk_k_g)
    q = jnp.concatenate([q[..., :MLA_NOPE], _rope(q[..., MLA_NOPE:], cos, sin)], axis=-1)
    k = jnp.concatenate([k[..., :MLA_NOPE], _rope(k[..., MLA_NOPE:], cos, sin)], axis=-1)
    o = _causal_attention(q, k, v)
    return o.reshape(b, t, MLA_OUT)


def _wkv7_scan(r, w, k, v, a, b):
    bsz = r.shape[0]

    def step(s, inp):
        r_t, w_t, k_t, v_t, a_t, b_t = inp
        sa = jnp.einsum('bhij,bhj->bhi', s, a_t)
        s = s * w_t[:, :, None, :] + sa[..., None] * b_t[:, :, None, :] + v_t[..., None] * k_t[:, :, None, :]
        return s, jnp.einsum('bhij,bhj->bhi', s, r_t)

    s0 = jnp.zeros((bsz, RWKV_HEADS, RWKV_HEAD, RWKV_HEAD), jnp.float32)
    xs = [z.swapaxes(0, 1) for z in (r, w, k, v, a, b)]
    _, y = lax.scan(step, s0, xs)
    return y.swapaxes(0, 1)


def _rwkv7_mix(u, mu, w0, w_up, a0, a_up, g_up, k_k, k_a, r_k, ln_g, ln_b):
    dt = u.dtype
    bsz, t, _ = u.shape
    f32 = jnp.float32
    uf = u.astype(f32)
    prev = jnp.pad(uf, ((0, 0), (1, 0), (0, 0)))[:, :-1]
    uf = uf + (prev - uf) * mu.astype(f32)
    s1 = RWKV_CH
    s3 = 3 * RWKV_CH
    r, k, v, wd, ad, gd = jnp.split(uf, [s1, 2 * s1, s3, s3 + DECAY_LORA, s3 + DECAY_LORA + ICLR_LORA], axis=-1)
    w_log = -jax.nn.softplus(-(w0.astype(f32) + jnp.tanh(wd) @ w_up.astype(f32))) - 0.5
    decay = jnp.exp(-jnp.exp(w_log))
    a = jax.nn.sigmoid(a0.astype(f32) + ad @ a_up.astype(f32))
    g = jax.nn.sigmoid(gd) @ g_up.astype(f32)
    hs = (bsz, t, RWKV_HEADS, RWKV_HEAD)
    kk = (k * k_k.astype(f32)).reshape(hs)
    kk = kk / jnp.maximum(jnp.linalg.norm(kk, axis=-1, keepdims=True), 1e-12)
    k = k * (1.0 + (a - 1.0) * k_a.astype(f32))
    r, k, v, decay, a = [z.reshape(hs) for z in (r, k, v, decay, a)]
    y = _wkv7_scan(r, decay, k, v, -kk, kk * a)
    y = _layer_norm(y, ln_g.reshape(RWKV_HEADS, RWKV_HEAD), ln_b.reshape(RWKV_HEADS, RWKV_HEAD), RWKV_GN_EPS)
    y = y + jnp.sum(r * k * r_k.astype(f32), axis=-1, keepdims=True) * v
    return (y.reshape(bsz, t, RWKV_CH) * g).astype(dt)


def _swiglu(h, w_gate, w_up, w_down):
    return (jax.nn.silu(h @ w_gate) * (h @ w_up)) @ w_down


def _moe_swiglu(h, router, w_gate, w_up, w_down):
    b, t, d = h.shape
    xf = h.reshape(-1, d)
    n_tok = xf.shape[0]
    probs = jax.nn.softmax((xf @ router).astype(jnp.float32), axis=-1)
    top_p, top_e = lax.top_k(probs, TOP_K)
    top_p = top_p / jnp.sum(top_p, axis=-1, keepdims=True)
    flat_e = top_e.reshape(-1)
    flat_p = top_p.reshape(-1)
    n_assign = n_tok * TOP_K
    order = jnp.argsort(flat_e)
    sorted_e = flat_e[order]
    counts = jnp.bincount(flat_e, length=N_EXPERTS)
    starts = jnp.cumsum(counts) - counts
    padded = (counts + MOE_BLOCK - 1) // MOE_BLOCK * MOE_BLOCK
    pad_ends = jnp.cumsum(padded)
    pad_starts = pad_ends - padded
    dest = pad_starts[sorted_e] + jnp.arange(n_assign) - starts[sorted_e]
    cap = -(-n_assign // MOE_BLOCK) * MOE_BLOCK + N_EXPERTS * MOE_BLOCK
    n_blk = cap // MOE_BLOCK
    slot_tok = jnp.zeros((cap,), jnp.int32).at[dest].set((order // TOP_K).astype(jnp.int32))
    slot_w = jnp.zeros((cap,), jnp.float32).at[dest].set(flat_p[order])
    blk_e = jnp.minimum(jnp.searchsorted(pad_ends, jnp.arange(n_blk) * MOE_BLOCK, side='right'), N_EXPERTS - 1)
    xs = xf[slot_tok].reshape(n_blk, MOE_BLOCK, d)

    def expert_block(args):
        xb, e = args
        return _swiglu(xb, w_gate[e], w_up[e], w_down[e])

    ys = lax.map(expert_block, (xs, blk_e)).reshape(cap, d)
    out = jnp.zeros((n_tok, d), jnp.float32).at[slot_tok].add(ys.astype(jnp.float32) * slot_w[:, None])
    return out.astype(h.dtype).reshape(b, t, d)


def setup_inputs(seed: int = 0) -> dict:
    key = jax.random.key(seed)
    ks = iter(jax.random.split(key, 48))
    L = DEPTH

    def nrm(shape, scale):
        return jax.random.normal(next(ks), shape, jnp.float32) * scale

    def gain(shape):
        return 1.0 + nrm(shape, 0.05)

    x = nrm((BATCH, SEQ, D_MODEL), 1.0)
    c = nrm((BATCH, D_MODEL), 1.0)
    positions = jnp.arange(SEQ, dtype=jnp.int32)[None, :] + jax.random.randint(next(ks), (BATCH, 1), 0, SEQ, dtype=jnp.int32)
    return {
        'x': x,
        'c': c,
        'positions': positions,
        'ada_w': nrm((D_MODEL, N_MOD * D_MODEL), 0.5 * D_MODEL ** -0.5),
        'ada_b': nrm((N_MOD * D_MODEL,), 0.02),
        'ada_layer_bias': nrm((L, N_MOD * D_MODEL), 0.02),
        'norm1_g': gain((L, D_MODEL)),
        'norm2_g': gain((L, D_MODEL)),
        'w_in': nrm((L, D_MODEL, IN_COLS), D_MODEL ** -0.5),
        'w_out': nrm((L, D_MIX, D_MODEL), D_MIX ** -0.5),
        'conv_w': nrm((L, CONV_K, CONV_CH), CONV_K ** -0.5),
        'conv_b': nrm((L, CONV_CH), 0.02),
        'conv_ln_g': gain((L, CONV_CH)),
        'conv_ln_b': nrm((L, CONV_CH), 0.02),
        'mla_q_norm_g': gain((L, MLA_Q_LORA)),
        'mla_w_uq': nrm((L, MLA_Q_LORA, MLA_HEADS * MLA_QK), MLA_Q_LORA ** -0.5),
        'mla_kv_norm_g': gain((L, MLA_KV_LORA)),
        'mla_w_ukv': nrm((L, MLA_KV_LORA, MLA_HEADS * (MLA_NOPE + MLA_V)), MLA_KV_LORA ** -0.5),
        'qk_norm_q': gain((L, MLA_QK)),
        'qk_norm_k': gain((L, MLA_QK)),
        'rwkv_mu': jax.random.uniform(next(ks), (L, RWKV_IN), jnp.float32, 0.0, 1.0),
        'rwkv_w0': nrm((L, RWKV_CH), 0.5) - 0.5,
        'rwkv_w_up': nrm((L, DECAY_LORA, RWKV_CH), 0.1),
        'rwkv_a0': nrm((L, RWKV_CH), 0.5),
        'rwkv_a_up': nrm((L, ICLR_LORA, RWKV_CH), 0.5 * ICLR_LORA ** -0.5),
        'rwkv_g_up': nrm((L, GATE_LORA, RWKV_CH), GATE_LORA ** -0.5),
        'rwkv_k_k': 0.85 + nrm((L, RWKV_CH), 0.05),
        'rwkv_k_a': gain((L, RWKV_CH)),
        'rwkv_r_k': nrm((L, RWKV_HEADS, RWKV_HEAD), 0.1),
        'rwkv_ln_g': gain((L, RWKV_CH)),
        'rwkv_ln_b': nrm((L, RWKV_CH), 0.02),
        'ffn_w_gate': nrm((N_DENSE, D_MODEL, D_FF), D_MODEL ** -0.5),
        'ffn_w_up': nrm((N_DENSE, D_MODEL, D_FF), D_MODEL ** -0.5),
        'ffn_w_down': nrm((N_DENSE, D_FF, D_MODEL), D_FF ** -0.5),
        'moe_router': nrm((N_MOE, D_MODEL, N_EXPERTS), D_MODEL ** -0.5),
        'moe_w_gate': nrm((N_MOE, N_EXPERTS, D_MODEL, MOE_FF), D_MODEL ** -0.5),
        'moe_w_up': nrm((N_MOE, N_EXPERTS, D_MODEL, MOE_FF), D_MODEL ** -0.5),
        'moe_w_down': nrm((N_MOE, N_EXPERTS, MOE_FF, D_MODEL), MOE_FF ** -0.5),
    }


def reference(x, c, positions, ada_w, ada_b, ada_layer_bias, norm1_g, norm2_g, w_in, w_out, conv_w, conv_b, conv_ln_g, conv_ln_b, mla_q_norm_g, mla_w_uq, mla_kv_norm_g, mla_w_ukv, qk_norm_q, qk_norm_k, rwkv_mu, rwkv_w0, rwkv_w_up, rwkv_a0, rwkv_a_up, rwkv_g_up, rwkv_k_k, rwkv_k_a, rwkv_r_k, rwkv_ln_g, rwkv_ln_b, ffn_w_gate, ffn_w_up, ffn_w_down, moe_router, moe_w_gate, moe_w_up, moe_w_down):
    inv_freq = ROPE_THETA ** (-jnp.arange(0, MLA_ROPE, 2, dtype=jnp.float32) / MLA_ROPE)
    ang = positions.astype(jnp.float32)[..., None] * inv_freq
    cos = jnp.cos(ang)[:, :, None, :]
    sin = jnp.sin(ang)[:, :, None, :]
    cond = jax.nn.silu(c) @ ada_w + ada_b
    for layer in range(DEPTH):
        mod = cond + ada_layer_bias[layer]
        sh1, sc1, gt1, sh2, sc2, gt2 = [m[:, None, :] for m in jnp.split(mod, N_MOD, axis=-1)]
        h = _rms_norm(x, norm1_g[layer]) * (1.0 + sc1) + sh1
        u = h @ w_in[layer]
        u_conv, u_mla, u_rwkv = jnp.split(u, [CONV_IN, CONV_IN + MLA_IN], axis=-1)
        y_conv = _conformer_conv(u_conv, conv_w[layer], conv_b[layer], conv_ln_g[layer], conv_ln_b[layer])
        y_mla = _mla(u_mla, cos, sin, mla_q_norm_g[layer], mla_w_uq[layer], mla_kv_norm_g[layer], mla_w_ukv[layer], qk_norm_q[layer], qk_norm_k[layer])
        y_rwkv = _rwkv7_mix(u_rwkv, rwkv_mu[layer], rwkv_w0[layer], rwkv_w_up[layer], rwkv_a0[layer], rwkv_a_up[layer], rwkv_g_up[layer], rwkv_k_k[layer], rwkv_k_a[layer], rwkv_r_k[layer], rwkv_ln_g[layer], rwkv_ln_b[layer])
        mixed = jnp.concatenate([y_conv, y_mla, y_rwkv], axis=-1)
        x = x + gt1 * (mixed @ w_out[layer])
        h = _rms_norm(x, norm2_g[layer]) * (1.0 + sc2) + sh2
        i = layer // 2
        if layer % 2 == 0:
            f = _swiglu(h, ffn_w_gate[i], ffn_w_up[i], ffn_w_down[i])
        else:
            f = _moe_swiglu(h, moe_router[i], moe_w_gate[i], moe_w_up[i], moe_w_down[i])
        x = x + gt2 * f
    return x
```

```python
import functools

import jax
import jax.numpy as jnp
from jax import lax
from jax.experimental import pallas as pl
from jax.experimental.pallas import tpu as pltpu

F32 = jnp.float32
BF16 = jnp.bfloat16
I32 = jnp.int32

D_MODEL = 1024
DEPTH = 4
N_MOD = 6
CONV_CH = 256
CONV_K = 31
CONV_IN = 2 * CONV_CH
MLA_NOPE = 64
MLA_ROPE = 32
MLA_V = 64
MLA_QK = MLA_NOPE + MLA_ROPE
MLA_HEADS = 8
MLA_OUT = MLA_HEADS * MLA_V
MLA_Q_LORA = 256
MLA_KV_LORA = 128
MLA_IN = MLA_Q_LORA + MLA_KV_LORA + MLA_ROPE
ROPE_THETA = 10000.0
RWKV_HEAD = 64
RWKV_CH = 256
RWKV_HEADS = 4
RWKV_IN = 1024
N_EXPERTS = 8
RMS_EPS = 1e-6
LN_EPS = 1e-5
RWKV_GN_EPS = 64e-5

LANES = 128
HEAD_PAD = LANES
MLA_PAD_IN = MLA_Q_LORA + MLA_KV_LORA + 2 * LANES
IN_PAD = CONV_IN + MLA_PAD_IN + RWKV_IN
CHUNK = 64
MOE_TILE = 512
VMEM_LIMIT = 56 * 1024 * 1024

_NEG = -1e30


def _cp(n_axes, vmem=None):
    return pltpu.CompilerParams(dimension_semantics=("arbitrary",) * n_axes,
                                vmem_limit_bytes=vmem if vmem else VMEM_LIMIT)


def _mm(a, b):
    return jnp.dot(a.astype(BF16), b.astype(BF16), preferred_element_type=F32)


def _mm_nt(a, b):
    return lax.dot_general(a.astype(BF16), b.astype(BF16), (((1,), (1,)), ((), ())),
                           preferred_element_type=F32)


def _mm_tn(a, b):
    return lax.dot_general(a.astype(BF16), b.astype(BF16), (((0,), (0,)), ((), ())),
                           preferred_element_type=F32)


def _sigmoid(x):
    return 1.0 / (1.0 + jnp.exp(-x))


def _split2(x):
    hi = x.astype(BF16)
    lo = (x - hi.astype(F32)).astype(BF16)
    return hi, lo


def _seg_sum(x, ones_bd):
    hi, lo = _split2(x)
    return (jnp.dot(hi, ones_bd, preferred_element_type=F32)
            + jnp.dot(lo, ones_bd, preferred_element_type=F32))


def _cond_kernel(c_ref, w_ref, b_ref, lb_ref, o_ref):
    c = c_ref[...]
    s = c * _sigmoid(c)
    base = jnp.dot(s, w_ref[...], precision=lax.Precision.HIGHEST, preferred_element_type=F32) + b_ref[...]
    for l in range(DEPTH):
        o_ref[l] = base + lb_ref[l]


def _cond(c, ada_w, ada_b, ada_layer_bias):
    b = c.shape[0]
    n = ada_w.shape[1]
    tn = 1536
    cp = jnp.zeros((8, D_MODEL), F32).at[:b].set(c)
    out = pl.pallas_call(
        _cond_kernel,
        grid=(n // tn,),
        in_specs=[pl.BlockSpec((8, D_MODEL), lambda j: (0, 0)),
                  pl.BlockSpec((D_MODEL, tn), lambda j: (0, j)),
                  pl.BlockSpec((1, tn), lambda j: (0, j)),
                  pl.BlockSpec((DEPTH, 1, tn), lambda j: (0, 0, j))],
        out_specs=pl.BlockSpec((DEPTH, 8, tn), lambda j: (0, 0, j)),
        out_shape=jax.ShapeDtypeStruct((DEPTH, 8, n), F32),
        compiler_params=_cp(1),
        name="cond",
    )(cp, ada_w, ada_b.reshape(1, n), ada_layer_bias.reshape(DEPTH, 1, n))
    return out[:, :b]


def _rope_kernel(pos_ref, invf_ref, c_ref, s_ref):
    pos = pos_ref[0].astype(F32)
    ang = pos * invf_ref[...]
    lane = lax.broadcasted_iota(I32, ang.shape, 1)
    cs = jnp.cos(ang)
    sn = jnp.sin(ang)
    c_ref[0] = jnp.where(lane < MLA_NOPE, 1.0, jnp.where(lane < MLA_QK, cs, 0.0))
    s_ref[0] = jnp.where((lane >= MLA_NOPE) & (lane < MLA_NOPE + 16), -sn,
                         jnp.where((lane >= MLA_NOPE + 16) & (lane < MLA_QK), sn, 0.0))


def _rope_tables(positions):
    b, t = positions.shape
    tt = min(512, t)
    inv_freq = ROPE_THETA ** (-jnp.arange(0, MLA_ROPE, 2, dtype=F32) / MLA_ROPE)
    invf = jnp.zeros((1, LANES), F32)
    invf = invf.at[0, MLA_NOPE:MLA_NOPE + 16].set(inv_freq).at[0, MLA_NOPE + 16:MLA_QK].set(inv_freq)
    spec = pl.BlockSpec((1, tt, LANES), lambda bi, i: (bi, i, 0))
    return pl.pallas_call(
        _rope_kernel,
        grid=(b, t // tt),
        in_specs=[pl.BlockSpec((1, tt, 1), lambda bi, i: (bi, i, 0)),
                  pl.BlockSpec((1, LANES), lambda bi, i: (0, 0))],
        out_specs=[spec, spec],
        out_shape=[jax.ShapeDtypeStruct((b, t, LANES), F32)] * 2,
        compiler_params=_cp(2),
        name="rope_tables",
    )(positions.reshape(b, t, 1), invf)


def _in_proj_kernel(x_ref, sc_ref, sh_ref, g_ref, w_ref, oc_ref, om_ref, or_ref):
    x = x_ref[...]
    ms = jnp.mean(x * x, axis=-1, keepdims=True)
    h = x * lax.rsqrt(ms + RMS_EPS) * g_ref[...]
    h = h * (1.0 + sc_ref[0]) + sh_ref[0]
    hb = h.astype(BF16)
    oc_ref[...] = jnp.dot(hb, w_ref[:, 0:CONV_IN], preferred_element_type=F32)
    om_ref[...] = jnp.dot(hb, w_ref[:, CONV_IN:CONV_IN + MLA_PAD_IN], preferred_element_type=F32)
    or_ref[...] = jnp.dot(hb, w_ref[:, CONV_IN + MLA_PAD_IN:IN_PAD], preferred_element_type=F32)


def _in_proj(x2, sc, sh, g, w, t):
    n = x2.shape[0]
    tm = min(512, t)
    tpb = t // tm
    vec = pl.BlockSpec((1, 1, D_MODEL), lambda i: (i // tpb, 0, 0))
    return pl.pallas_call(
        _in_proj_kernel,
        grid=(n // tm,),
        in_specs=[pl.BlockSpec((tm, D_MODEL), lambda i: (i, 0)), vec, vec,
                  pl.BlockSpec((1, D_MODEL), lambda i: (0, 0)),
                  pl.BlockSpec((D_MODEL, IN_PAD), lambda i: (0, 0))],
        out_specs=[pl.BlockSpec((tm, CONV_IN), lambda i: (i, 0)),
                   pl.BlockSpec((tm, MLA_PAD_IN), lambda i: (i, 0)),
                   pl.BlockSpec((tm, RWKV_IN), lambda i: (i, 0))],
        out_shape=[jax.ShapeDtypeStruct((n, CONV_IN), F32),
                   jax.ShapeDtypeStruct((n, MLA_PAD_IN), F32),
                   jax.ShapeDtypeStruct((n, RWKV_IN), F32)],
        compiler_params=_cp(1),
        name="in_proj",
    )(x2, sc, sh, g, w)


_CONV_HALO = 32


def _conv_kernel(u_ref, up_ref, w_ref, b_ref, g_ref, bb_ref, o_ref, zbuf, *, tt):
    i = pl.program_id(1)
    u = u_ref[0]
    z = u[:, :CONV_CH] * _sigmoid(u[:, CONV_CH:])
    up = up_ref[0]
    zp = up[:, :CONV_CH] * _sigmoid(up[:, CONV_CH:])
    zp = jnp.where(i > 0, zp, 0.0)
    zbuf[0:_CONV_HALO, :] = zp
    zbuf[_CONV_HALO:, :] = z
    acc = jnp.zeros((tt, CONV_CH), F32)
    for k in range(CONV_K):
        acc = acc + zbuf[pl.ds(_CONV_HALO - (CONV_K - 1) + k, tt), :] * w_ref[k:k + 1, :]
    y = acc + b_ref[...]
    mu = jnp.mean(y, axis=-1, keepdims=True)
    yc = y - mu
    var = jnp.mean(yc * yc, axis=-1, keepdims=True)
    yn = yc * lax.rsqrt(var + LN_EPS) * g_ref[...] + bb_ref[...]
    o_ref[0] = yn * _sigmoid(yn)


def _conformer_conv(u3, conv_w, conv_b, ln_g, ln_b):
    b, t, _ = u3.shape
    tt = min(512, t)
    hb = tt // _CONV_HALO
    vec = pl.BlockSpec((1, CONV_CH), lambda bi, i: (0, 0))
    return pl.pallas_call(
        functools.partial(_conv_kernel, tt=tt),
        grid=(b, t // tt),
        in_specs=[pl.BlockSpec((1, tt, CONV_IN), lambda bi, i: (bi, i, 0)),
                  pl.BlockSpec((1, _CONV_HALO, CONV_IN), lambda bi, i: (bi, jnp.maximum(i * hb - 1, 0), 0)),
                  pl.BlockSpec((CONV_K, CONV_CH), lambda bi, i: (0, 0)), vec, vec, vec],
        out_specs=pl.BlockSpec((1, tt, CONV_CH), lambda bi, i: (bi, i, 0)),
        out_shape=jax.ShapeDtypeStruct((b, t, CONV_CH), F32),
        scratch_shapes=[pltpu.VMEM((tt + _CONV_HALO, CONV_CH), F32)],
        compiler_params=_cp(2),
        name="conformer_conv",
    )(u3, u3, conv_w, conv_b.reshape(1, -1), ln_g.reshape(1, -1), ln_b.reshape(1, -1))


def _mla_prep_kernel(um_ref, c_ref, s_ref, gql_ref, wq_ref, wqs_ref, gkvl_ref, wk_ref, wv_ref,
                     gq_ref, gqs_ref, gk_ref, gks_ref, q_ref, kt_ref, v_ref):
    um = um_ref[0]
    cq = um[:, 0:MLA_Q_LORA]
    ckv = um[:, MLA_Q_LORA:MLA_Q_LORA + MLA_KV_LORA]
    kr = um[:, MLA_Q_LORA + MLA_KV_LORA:MLA_Q_LORA + MLA_KV_LORA + LANES]
    krs = um[:, MLA_Q_LORA + MLA_KV_LORA + LANES:MLA_PAD_IN]
    cqn = (cq * lax.rsqrt(jnp.mean(cq * cq, axis=-1, keepdims=True) + RMS_EPS) * gql_ref[...]).astype(BF16)
    ckvn = (ckv * lax.rsqrt(jnp.mean(ckv * ckv, axis=-1, keepdims=True) + RMS_EPS) * gkvl_ref[...]).astype(BF16)
    q = jnp.dot(cqn, wq_ref[...], preferred_element_type=F32)
    qs = jnp.dot(cqn, wqs_ref[...], preferred_element_type=F32)
    kn = jnp.dot(ckvn, wk_ref[...], preferred_element_type=F32)
    vv = jnp.dot(ckvn, wv_ref[...], preferred_element_type=F32)
    cos_t = c_ref[0]
    sin_t = s_ref[0]
    lane = lax.broadcasted_iota(I32, cos_t.shape, 1)
    scale = float(MLA_QK) ** -0.5
    inv_d = 1.0 / MLA_QK
    for h in range(MLA_HEADS):
        sl = slice(h * HEAD_PAD, (h + 1) * HEAD_PAD)
        qh = q[:, sl]
        r = lax.rsqrt(jnp.sum(qh * qh, axis=-1, keepdims=True) * inv_d + RMS_EPS)
        qf = qh * (r * gq_ref[...]) * cos_t + qs[:, sl] * (r * gqs_ref[...]) * sin_t
        q_ref[0, :, sl] = (qf * scale).astype(BF16)
        kh = kn[:, sl] + kr
        r = lax.rsqrt(jnp.sum(kh * kh, axis=-1, keepdims=True) * inv_d + RMS_EPS)
        kf = kh * (r * gk_ref[...]) * cos_t + krs * (r * gks_ref[...]) * sin_t
        kt_ref[0, h] = kf.T.astype(BF16)
        v_ref[0, h] = jnp.where(lane == MLA_V, 1.0, vv[:, sl]).astype(BF16)


def _mla_prep(um3, cos_t, sin_t, p):
    b, t, _ = um3.shape
    tt = min(256, t)
    full = lambda shape: pl.BlockSpec(shape, lambda bi, i: (0,) * len(shape))
    tab = pl.BlockSpec((1, tt, LANES), lambda bi, i: (bi, i, 0))
    return pl.pallas_call(
        _mla_prep_kernel,
        grid=(b, t // tt),
        in_specs=[pl.BlockSpec((1, tt, MLA_PAD_IN), lambda bi, i: (bi, i, 0)), tab, tab,
                  full((1, MLA_Q_LORA)), full((MLA_Q_LORA, MLA_HEADS * HEAD_PAD)),
                  full((MLA_Q_LORA, MLA_HEADS * HEAD_PAD)),
                  full((1, MLA_KV_LORA)), full((MLA_KV_LORA, MLA_HEADS * HEAD_PAD)),
                  full((MLA_KV_LORA, MLA_HEADS * HEAD_PAD)),
                  full((1, LANES)), full((1, LANES)), full((1, LANES)), full((1, LANES))],
        out_specs=[pl.BlockSpec((1, tt, MLA_HEADS * HEAD_PAD), lambda bi, i: (bi, i, 0)),
                   pl.BlockSpec((1, MLA_HEADS, HEAD_PAD, tt), lambda bi, i: (bi, 0, 0, i)),
                   pl.BlockSpec((1, MLA_HEADS, tt, HEAD_PAD), lambda bi, i: (bi, 0, i, 0))],
        out_shape=[jax.ShapeDtypeStruct((b, t, MLA_HEADS * HEAD_PAD), BF16),
                   jax.ShapeDtypeStruct((b, MLA_HEADS, HEAD_PAD, t), BF16),
                   jax.ShapeDtypeStruct((b, MLA_HEADS, t, HEAD_PAD), BF16)],
        compiler_params=_cp(2),
        name="mla_prep",
    )(um3, cos_t, sin_t, p["gql"], p["wq"], p["wqs"], p["gkvl"], p["wk"], p["wv"],
      p["gq"], p["gqs"], p["gk"], p["gks"])


def _attn_kernel(q_ref, kt_ref, v_ref, o_ref, *, tq, tk):
    i = pl.program_id(1)
    nfull = i * (tq // tk)
    row = lax.broadcasted_iota(I32, (tq, tk), 0)
    col = lax.broadcasted_iota(I32, (tq, tk), 1)
    outs = []
    for h in range(MLA_HEADS):
        q = q_ref[0, :, h * HEAD_PAD:(h + 1) * HEAD_PAD]

        def step(off, m, acc, mask):
            s = jnp.dot(q, kt_ref[0, h, :, pl.ds(off, tk)], preferred_element_type=F32)
            if mask is not None:
                s = jnp.where(mask, s, _NEG)
            m_new = jnp.maximum(m, jnp.max(s, axis=-1, keepdims=True))
            p = jnp.exp(s - m_new)
            acc = acc * jnp.exp(m - m_new) + jnp.dot(p.astype(BF16), v_ref[0, h, pl.ds(off, tk), :],
                                                     preferred_element_type=F32)
            return m_new, acc

        def body(kb, carry):
            return step(pl.multiple_of(kb * tk, tk), carry[0], carry[1], None)

        m, acc = lax.fori_loop(0, nfull, body,
                               (jnp.full((tq, 1), _NEG, F32), jnp.zeros((tq, HEAD_PAD), F32)))
        for d in range(tq // tk):
            off = pl.multiple_of(i * tq + d * tk, tk)
            m, acc = step(off, m, acc, col + d * tk <= row)
        outs.append(acc[:, :MLA_V] / acc[:, MLA_V:MLA_V + 1])
    o_ref[0] = jnp.concatenate(outs, axis=-1)


def _attention(q, kt, v):
    b, t, _ = q.shape
    tq = min(256, t)
    tk = tq
    return pl.pallas_call(
        functools.partial(_attn_kernel, tq=tq, tk=tk),
        grid=(b, t // tq),
        in_specs=[pl.BlockSpec((1, tq, MLA_HEADS * HEAD_PAD), lambda bi, i: (bi, i, 0)),
                  pl.BlockSpec((1, MLA_HEADS, HEAD_PAD, t), lambda bi, i: (bi, 0, 0, 0)),
                  pl.BlockSpec((1, MLA_HEADS, t, HEAD_PAD), lambda bi, i: (bi, 0, 0, 0))],
        out_specs=pl.BlockSpec((1, tq, MLA_OUT), lambda bi, i: (bi, i, 0)),
        out_shape=jax.ShapeDtypeStruct((b, t, MLA_OUT), F32),
        compiler_params=_cp(2),
        name="mla_attention",
    )(q, kt, v)


def _rwkv_prep_kernel(u_ref, up_ref, mu_ref, w0_ref, wup_ref, a0_ref, aup_ref, gup_ref, kk_ref, ka_ref,
                      rk_ref, ones_ref, r_o, lw_o, k_o, v_o, a_o, b_o, bonus_o, g_o):
    i = pl.program_id(1)
    u = u_ref[0]
    prow = jnp.where(i > 0, up_ref[0][7:8, :], 0.0)
    rowi = lax.broadcasted_iota(I32, u.shape, 0)
    prev = jnp.where(rowi == 0, prow, pltpu.roll(u, 1, axis=0))
    uf = u + (prev - u) * mu_ref[...]
    c = RWKV_CH
    r = uf[:, 0:c]
    k = uf[:, c:2 * c]
    v = uf[:, 2 * c:3 * c]
    wa = uf[:, 3 * c:3 * c + LANES]
    gd = uf[:, 3 * c + LANES:]
    ones_bd = ones_ref[...]
    wl = w0_ref[...] + _mm(jnp.tanh(wa), wup_ref[...])
    z = -wl
    softplus = jnp.maximum(z, 0.0) + jnp.log(1.0 + jnp.exp(-jnp.abs(z)))
    lw = -jnp.exp(-softplus - 0.5)
    a_s = _sigmoid(a0_ref[...] + _mm(wa, aup_ref[...]))
    g = _mm(_sigmoid(gd), gup_ref[...])
    kk = k * kk_ref[...]
    nrm = jnp.maximum(jnp.sqrt(_seg_sum(kk * kk, ones_bd)), 1e-12)
    kkn = kk / nrm
    k2 = k * (1.0 + (a_s - 1.0) * ka_ref[...])
    r_o[0] = r
    lw_o[0] = lw
    k_o[0] = k2
    v_o[0] = v
    a_o[0] = -kkn
    b_o[0] = kkn * a_s
    bonus_o[0] = _seg_sum(r * k2 * rk_ref[...], ones_bd) * v
    g_o[0] = g


def _rwkv_prep(u3, p):
    b, t, _ = u3.shape
    tt = min(512, t)
    hb = tt // 8
    full = lambda shape: pl.BlockSpec(shape, lambda bi, i: (0,) * len(shape))
    out = pl.BlockSpec((1, tt, RWKV_CH), lambda bi, i: (bi, i, 0))
    return pl.pallas_call(
        _rwkv_prep_kernel,
        grid=(b, t // tt),
        in_specs=[pl.BlockSpec((1, tt, RWKV_IN), lambda bi, i: (bi, i, 0)),
                  pl.BlockSpec((1, 8, RWKV_IN), lambda bi, i: (bi, jnp.maximum(i * hb - 1, 0), 0)),
                  full((1, RWKV_IN)), full((1, RWKV_CH)), full((LANES, RWKV_CH)), full((1, RWKV_CH)),
                  full((LANES, RWKV_CH)), full((LANES, RWKV_CH)), full((1, RWKV_CH)), full((1, RWKV_CH)),
                  full((1, RWKV_CH)), full((RWKV_CH, RWKV_CH))],
        out_specs=[out] * 8,
        out_shape=[jax.ShapeDtypeStruct((b, t, RWKV_CH), F32)] * 8,
        compiler_params=_cp(2),
        name="rwkv_prep",
    )(u3, u3, p["mu"], p["w0"], p["wup"], p["a0"], p["aup"], p["gup"], p["kk"], p["ka"], p["rk"], p["ones"])


def _intra_chunk(r, lw, k, v, a, b, mask_bd, eye_bd, tril, strict, incl, eye_p):
    c = CHUNK
    lw_hi = lw.astype(BF16)
    rem = lw - lw_hi.astype(F32)
    lw_mid = rem.astype(BF16)
    lw_lo = (rem - lw_mid.astype(F32)).astype(BF16)
    cum = (jnp.dot(tril, lw_hi, preferred_element_type=F32)
           + jnp.dot(tril, lw_mid, preferred_element_type=F32)
           + jnp.dot(tril, lw_lo, preferred_element_type=F32))
    tot = cum[c - 1:c, :]
    e_neg = jnp.exp(-cum)
    e_end = jnp.exp(tot - cum)
    rt = r * jnp.exp(cum)
    at = a * jnp.exp(cum - lw)
    bt = b * e_neg
    kt = k * e_neg
    bh = b * e_end
    kh = k * e_end

    def bd(x):
        xb = x.astype(BF16)
        return jnp.where(mask_bd, jnp.concatenate([xb, xb, xb, xb], axis=0), jnp.zeros((), BF16))

    stack = jnp.concatenate([at, rt], axis=0)
    s1 = _mm_nt(stack, bd(bt))
    s2 = _mm_nt(stack, bd(kt))
    a_ab = jnp.where(strict, s1[:c], 0.0)
    m_rb = jnp.where(incl, s1[c:], 0.0)
    a_ak = jnp.where(strict, s2[:c], 0.0)
    m_rk = jnp.where(incl, s2[c:], 0.0)
    x = eye_p + a_ab
    ak = _mm(a_ab, bd(a_ab))
    for j in range(5):
        if j < 4:
            pp = _mm(jnp.concatenate([ak, x], axis=0), bd(ak))
            x = x + pp[c:]
            ak = pp[:c]
        else:
            x = x + _mm(x, bd(ak))
    bdv = bd(v)
    av = _mm(a_ak, bdv)
    ah = _mm(x, bd(at))
    uh = _mm(x, bd(av))
    q = rt + _mm(m_rb, bd(ah))
    y0 = _mm(m_rb, bd(uh)) + _mm(m_rk, bdv)
    g = jnp.where(mask_bd, _mm_tn(bh, ah), 0.0) + jnp.where(eye_bd, jnp.exp(tot), 0.0)
    f = jnp.where(mask_bd, _mm_tn(bh, uh) + _mm_tn(kh, v), 0.0)
    return q, y0, g, f


def _rwkv_intra_kernel(r_ref, lw_ref, k_ref, v_ref, a_ref, b_ref, q_o, y0_o, g_o, f_o, *, cps):
    c = CHUNK
    n = RWKV_CH
    ri = lax.broadcasted_iota(I32, (n, n), 0)
    ci = lax.broadcasted_iota(I32, (n, n), 1)
    mask_bd = (ri // RWKV_HEAD) == (ci // RWKV_HEAD)
    eye_bd = ri == ci
    t64 = lax.broadcasted_iota(I32, (c, c), 0)
    i64 = lax.broadcasted_iota(I32, (c, c), 1)
    tril = jnp.where(i64 <= t64, 1.0, 0.0).astype(BF16)
    tp = lax.broadcasted_iota(I32, (c, n), 0)
    ip = lax.broadcasted_iota(I32, (c, n), 1) & (RWKV_HEAD - 1)
    strict = ip < tp
    incl = ip <= tp
    eye_p = jnp.where(ip == tp, 1.0, 0.0)
    for s in range(cps):
        sl = slice(s * c, (s + 1) * c)
        q, y0, g, f = _intra_chunk(r_ref[0, sl, :], lw_ref[0, sl, :], k_ref[0, sl, :], v_ref[0, sl, :],
                                   a_ref[0, sl, :], b_ref[0, sl, :], mask_bd, eye_bd, tril, strict, incl, eye_p)
        q_o[0, sl, :] = q
        y0_o[0, sl, :] = y0
        g_o[0, s] = g
        f_o[0, s] = f


def _rwkv_intra(r, lw, k, v, a, b_):
    b, t, n = r.shape
    nc = t // CHUNK
    cps = 2 if nc % 2 == 0 else 1
    tok = pl.BlockSpec((1, cps * CHUNK, n), lambda bi, i: (bi, i, 0))
    mat = pl.BlockSpec((1, cps, n, n), lambda bi, i: (bi, i, 0, 0))
    return pl.pallas_call(
        functools.partial(_rwkv_intra_kernel, cps=cps),
        grid=(b, nc // cps),
        in_specs=[tok] * 6,
        out_specs=[tok, tok, mat, mat],
        out_shape=[jax.ShapeDtypeStruct((b, t, n), F32), jax.ShapeDtypeStruct((b, t, n), F32),
                   jax.ShapeDtypeStruct((b, nc, n, n), F32), jax.ShapeDtypeStruct((b, nc, n, n), F32)],
        compiler_params=_cp(2),
        name="rwkv_intra",
    )(r, lw, k, v, a, b_)


def _rwkv_seq_kernel(q_ref, y0_ref, g_ref, f_ref, bonus_ref, gate_ref, lng_ref, lnb_ref, ones_ref, o_ref, h_ref,
                     *, nb):
    c = pl.program_id(0)

    @pl.when(c == 0)
    def _():
        h_ref[...] = jnp.zeros_like(h_ref)

    ones_bd = ones_ref[...]
    inv = 1.0 / RWKV_HEAD
    for b in range(nb):
        hb = h_ref[b].astype(BF16)
        y = jnp.dot(q_ref[b].astype(BF16), hb, preferred_element_type=F32) + y0_ref[b]
        h_ref[b] = jnp.dot(g_ref[b, 0].astype(BF16), hb, preferred_element_type=F32) + f_ref[b, 0]
        mu = _seg_sum(y, ones_bd) * inv
        yc = y - mu
        var = _seg_sum(yc * yc, ones_bd) * inv
        yn = yc * lax.rsqrt(var + RWKV_GN_EPS) * lng_ref[...] + lnb_ref[...]
        o_ref[b] = (yn + bonus_ref[b]) * gate_ref[b]


def _rwkv_seq(q, y0, g, f, bonus, gate, ln_g, ln_b, ones_bd):
    b, t, n = q.shape
    nc = t // CHUNK
    tok = pl.BlockSpec((b, CHUNK, n), lambda c: (0, c, 0))
    mat = pl.BlockSpec((b, 1, n, n), lambda c: (0, c, 0, 0))
    vec = pl.BlockSpec((1, n), lambda c: (0, 0))
    return pl.pallas_call(
        functools.partial(_rwkv_seq_kernel, nb=b),
        grid=(nc,),
        in_specs=[tok, tok, mat, mat, tok, tok, vec, vec, pl.BlockSpec((n, n), lambda c: (0, 0))],
        out_specs=tok,
        out_shape=jax.ShapeDtypeStruct((b, t, n), F32),
        scratch_shapes=[pltpu.VMEM((b, n, n), F32)],
        compiler_params=_cp(1),
        name="rwkv_seq",
    )(q, y0, g, f, bonus, gate, ln_g, ln_b, ones_bd)


def _out_proj_kernel(x_ref, yc_ref, ym_ref, yr_ref, w_ref, gt_ref, g2_ref, sc_ref, sh_ref, *rest, with_router):
    if with_router:
        wr_ref, xo_ref, h_ref, lg_ref = rest
    else:
        xo_ref, h_ref = rest
    mixed = (_mm(yc_ref[...], w_ref[0:CONV_CH, :])
             + _mm(ym_ref[...], w_ref[CONV_CH:CONV_CH + MLA_OUT, :])
             + _mm(yr_ref[...], w_ref[CONV_CH + MLA_OUT:, :]))
    x = x_ref[...] + gt_ref[0] * mixed
    xo_ref[...] = x
    ms = jnp.mean(x * x, axis=-1, keepdims=True)
    h = x * lax.rsqrt(ms + RMS_EPS) * g2_ref[...]
    h = h * (1.0 + sc_ref[0]) + sh_ref[0]
    h_ref[...] = h
    if with_router:
        lg_ref[...] = jnp.dot(h, wr_ref[...], precision=lax.Precision.HIGHEST, preferred_element_type=F32)


def _out_proj(x2, yc, ym, yr, w_out, gt, g2, sc, sh, router, t):
    n = x2.shape[0]
    tm = min(512, t)
    tpb = t // tm
    vec = pl.BlockSpec((1, 1, D_MODEL), lambda i: (i // tpb, 0, 0))
    row = lambda w: pl.BlockSpec((tm, w), lambda i: (i, 0))
    in_specs = [row(D_MODEL), row(CONV_CH), row(MLA_OUT), row(RWKV_CH),
                pl.BlockSpec((D_MODEL, D_MODEL), lambda i: (0, 0)), vec,
                pl.BlockSpec((1, D_MODEL), lambda i: (0, 0)), vec, vec]
    out_specs = [row(D_MODEL), row(D_MODEL)]
    out_shape = [jax.ShapeDtypeStruct((n, D_MODEL), F32), jax.ShapeDtypeStruct((n, D_MODEL), F32)]
    args = [x2, yc, ym, yr, w_out, gt, g2, sc, sh]
    if router is not None:
        in_specs.append(pl.BlockSpec((D_MODEL, LANES), lambda i: (0, 0)))
        out_specs.append(row(LANES))
        out_shape.append(jax.ShapeDtypeStruct((n, LANES), F32))
        args.append(router)
    return pl.pallas_call(
        functools.partial(_out_proj_kernel, with_router=router is not None),
        grid=(n // tm,),
        in_specs=in_specs, out_specs=out_specs, out_shape=out_shape,
        compiler_params=_cp(1),
        name="out_proj",
    )(*args)


def _swiglu_step(h_ref, wg_ref, wu_ref, wd_ref, hb_ref, acc_ref, j):
    @pl.when(j == 0)
    def _():
        hb_ref[...] = h_ref[...].astype(BF16)
        acc_ref[...] = jnp.zeros_like(acc_ref)

    hb = hb_ref[...]
    gate = jnp.dot(hb, wg_ref[...], preferred_element_type=F32)
    up = jnp.dot(hb, wu_ref[...], preferred_element_type=F32)
    act = (gate * _sigmoid(gate) * up).astype(BF16)
    acc_ref[...] += jnp.dot(act, wd_ref[...], preferred_element_type=F32)


def _ffn_kernel(h_ref, x_ref, gt_ref, wg_ref, wu_ref, wd_ref, o_ref, hb_ref, acc_ref, *, nj):
    j = pl.program_id(1)
    _swiglu_step(h_ref, wg_ref, wu_ref, wd_ref, hb_ref, acc_ref, j)

    @pl.when(j == nj - 1)
    def _():
        o_ref[...] = x_ref[...] + gt_ref[0] * acc_ref[...]


def _dense_ffn(h2, x2, gt, wg, wu, wd, t):
    n = x2.shape[0]
    ff = wg.shape[1]
    tm = min(512, t)
    tpb = t // tm
    nj = 2
    tf = ff // nj
    row = pl.BlockSpec((tm, D_MODEL), lambda i, j: (i, 0))
    return pl.pallas_call(
        functools.partial(_ffn_kernel, nj=nj),
        grid=(n // tm, nj),
        in_specs=[row, row, pl.BlockSpec((1, 1, D_MODEL), lambda i, j: (i // tpb, 0, 0)),
                  pl.BlockSpec((D_MODEL, tf), lambda i, j: (0, j)),
                  pl.BlockSpec((D_MODEL, tf), lambda i, j: (0, j)),
                  pl.BlockSpec((tf, D_MODEL), lambda i, j: (j, 0))],
        out_specs=row,
        out_shape=jax.ShapeDtypeStruct((n, D_MODEL), F32),
        scratch_shapes=[pltpu.VMEM((tm, D_MODEL), BF16), pltpu.VMEM((tm, D_MODEL), F32)],
        compiler_params=_cp(2),
        name="dense_ffn",
    )(h2, x2, gt, wg, wu, wd)


def _moe_ffn_kernel(te_ref, nu_ref, xs_ref, wg_ref, wu_ref, wd_ref, o_ref, hb_ref, acc_ref, *, nj):
    i = pl.program_id(0)
    j = pl.program_id(1)
    used = i < nu_ref[0]

    @pl.when(used)
    def _():
        _swiglu_step(xs_ref, wg_ref.at[0], wu_ref.at[0], wd_ref.at[0], hb_ref, acc_ref, j)

    @pl.when(j == nj - 1)
    def _():
        o_ref[...] = jnp.where(used, acc_ref[...], 0.0)


def _moe_ffn(xs, tile_e, n_used, wg, wu, wd):
    cap = xs.shape[0]
    ff = wg.shape[2]
    nj = 2
    tf = ff // nj
    tm = MOE_TILE

    def wj(i, j, nu):
        return jnp.where(i < nu[0], j, nj - 1)

    row = pl.BlockSpec((tm, D_MODEL), lambda i, j, te, nu: (i, 0))
    grid_spec = pltpu.PrefetchScalarGridSpec(
        num_scalar_prefetch=2,
        grid=(cap // tm, nj),
        in_specs=[row,
                  pl.BlockSpec((1, D_MODEL, tf), lambda i, j, te, nu: (te[i], 0, wj(i, j, nu))),
                  pl.BlockSpec((1, D_MODEL, tf), lambda i, j, te, nu: (te[i], 0, wj(i, j, nu))),
                  pl.BlockSpec((1, tf, D_MODEL), lambda i, j, te, nu: (te[i], wj(i, j, nu), 0))],
        out_specs=row,
        scratch_shapes=[pltpu.VMEM((tm, D_MODEL), BF16), pltpu.VMEM((tm, D_MODEL), F32)])
    return pl.pallas_call(
        functools.partial(_moe_ffn_kernel, nj=nj),
        grid_spec=grid_spec,
        out_shape=jax.ShapeDtypeStruct((cap, D_MODEL), F32),
        compiler_params=_cp(2),
        name="moe_ffn",
    )(tile_e, n_used, xs, wg, wu, wd)


def _route_kernel(lg_ref, oi_ref, of_ref, cnt_ref, carry_ref, *, tm):
    i = pl.program_id(0)

    @pl.when(i == 0)
    def _():
        carry_ref[...] = jnp.zeros_like(carry_ref)

    lane = lax.broadcasted_iota(I32, (tm, LANES), 1)
    real = lane < N_EXPERTS
    lg = jnp.where(real, lg_ref[...], _NEG)
    ex = jnp.exp(lg - jnp.max(lg, axis=-1, keepdims=True))
    p = ex / jnp.sum(ex, axis=-1, keepdims=True)
    p = jnp.where(real, p, -1.0)
    m1 = jnp.max(p, axis=-1, keepdims=True)
    i1 = jnp.min(jnp.where(p == m1, lane, LANES), axis=-1, keepdims=True)
    p2 = jnp.where(lane == i1, -1.0, p)
    m2 = jnp.max(p2, axis=-1, keepdims=True)
    i2 = jnp.min(jnp.where(p2 == m2, lane, LANES), axis=-1, keepdims=True)
    den = m1 + m2
    oh1 = lane == i1
    oh2 = lane == i2
    oh = jnp.where(oh1 | oh2, 1.0, 0.0)
    rr = lax.broadcasted_iota(I32, (tm, tm), 0)
    cc = lax.broadcasted_iota(I32, (tm, tm), 1)
    lstrict = jnp.where(cc < rr, 1.0, 0.0).astype(BF16)
    pre = jnp.dot(lstrict, oh.astype(BF16), preferred_element_type=F32) + carry_ref[...]
    rank1 = jnp.sum(jnp.where(oh1, pre, 0.0), axis=-1, keepdims=True).astype(I32)
    rank2 = jnp.sum(jnp.where(oh2, pre, 0.0), axis=-1, keepdims=True).astype(I32)
    carry_ref[...] = carry_ref[...] + jnp.sum(oh, axis=0, keepdims=True)
    cnt_ref[...] = carry_ref[...]
    oi_ref[...] = jnp.where(lane == 0, i1, jnp.where(lane == 1, i2, jnp.where(lane == 2, rank1,
                                                                                jnp.where(lane == 3, rank2, 0))))
    of_ref[...] = jnp.where(lane == 0, m1 / den, jnp.where(lane == 1, m2 / den, 0.0))


def _route(logits, t):
    n = logits.shape[0]
    tm = min(512, t)
    row = pl.BlockSpec((tm, LANES), lambda i: (i, 0))
    return pl.pallas_call(
        functools.partial(_route_kernel, tm=tm),
        grid=(n // tm,),
        in_specs=[row],
        out_specs=[row, row, pl.BlockSpec((1, LANES), lambda i: (0, 0))],
        out_shape=[jax.ShapeDtypeStruct((n, LANES), I32), jax.ShapeDtypeStruct((n, LANES), F32),
                   jax.ShapeDtypeStruct((1, LANES), F32)],
        scratch_shapes=[pltpu.VMEM((1, LANES), F32)],
        compiler_params=_cp(1),
        name="moe_route",
    )(logits)


def _dispatch_kernel(s0_ref, s1_ref, h_ref, init_ref, xs_ref, sem, *, tm):
    del init_ref
    base = pl.program_id(0) * tm

    def row_copy(r, slot):
        return pltpu.make_async_copy(h_ref.at[pl.ds(r, 1)], xs_ref.at[pl.ds(slot, 1)], sem)

    def start(r, carry):
        row_copy(r, s0_ref[base + r]).start()
        row_copy(r, s1_ref[base + r]).start()
        return carry

    lax.fori_loop(0, tm, start, 0)

    def wait(r, carry):
        row_copy(r, s0_ref[base + r]).wait()
        row_copy(r, s1_ref[base + r]).wait()
        return carry

    lax.fori_loop(0, tm, wait, 0)


def _dispatch(h2, slot0, slot1, cap, t):
    n = h2.shape[0]
    tm = min(512, t)
    grid_spec = pltpu.PrefetchScalarGridSpec(
        num_scalar_prefetch=2,
        grid=(n // tm,),
        in_specs=[pl.BlockSpec((tm, D_MODEL), lambda i, s0, s1: (i, 0)),
                  pl.BlockSpec(memory_space=pl.ANY)],
        out_specs=pl.BlockSpec(memory_space=pl.ANY),
        scratch_shapes=[pltpu.SemaphoreType.DMA(())])
    return pl.pallas_call(
        functools.partial(_dispatch_kernel, tm=tm),
        grid_spec=grid_spec,
        out_shape=jax.ShapeDtypeStruct((cap, D_MODEL), F32),
        input_output_aliases={3: 0},
        compiler_params=_cp(1),
        name="moe_dispatch",
    )(slot0, slot1, h2, jnp.zeros((cap, D_MODEL), F32))


def _combine_kernel(s0_ref, s1_ref, x_ref, wf_ref, gt_ref, ys_ref, o_ref, buf0, buf1, sem, *, tm):
    base = pl.program_id(0) * tm

    def row_copy(slot, buf, r):
        return pltpu.make_async_copy(ys_ref.at[pl.ds(slot, 1)], buf.at[pl.ds(r, 1)], sem)

    def start(r, carry):
        row_copy(s0_ref[base + r], buf0, r).start()
        row_copy(s1_ref[base + r], buf1, r).start()
        return carry

    lax.fori_loop(0, tm, start, 0)

    def wait(r, carry):
        row_copy(s0_ref[base + r], buf0, r).wait()
        row_copy(s1_ref[base + r], buf1, r).wait()
        return carry

    lax.fori_loop(0, tm, wait, 0)
    wf = wf_ref[...]
    f = wf[:, 0:1] * buf0[...] + wf[:, 1:2] * buf1[...]
    o_ref[...] = x_ref[...] + gt_ref[0] * f


def _combine(x2, ys, slot0, slot1, wf, gt, t):
    n = x2.shape[0]
    tm = min(512, t)
    tpb = t // tm
    row = pl.BlockSpec((tm, D_MODEL), lambda i, s0, s1: (i, 0))
    grid_spec = pltpu.PrefetchScalarGridSpec(
        num_scalar_prefetch=2,
        grid=(n // tm,),
        in_specs=[row, pl.BlockSpec((tm, LANES), lambda i, s0, s1: (i, 0)),
                  pl.BlockSpec((1, 1, D_MODEL), lambda i, s0, s1: (i // tpb, 0, 0)),
                  pl.BlockSpec(memory_space=pl.ANY)],
        out_specs=row,
        scratch_shapes=[pltpu.VMEM((tm, D_MODEL), F32), pltpu.VMEM((tm, D_MODEL), F32),
                        pltpu.SemaphoreType.DMA(())])
    return pl.pallas_call(
        functools.partial(_combine_kernel, tm=tm),
        grid_spec=grid_spec,
        out_shape=jax.ShapeDtypeStruct((n, D_MODEL), F32),
        compiler_params=_cp(1),
        name="moe_combine",
    )(slot0, slot1, x2, wf, gt, ys)


def _moe(h2, x2, logits, gt, wg, wu, wd, t):
    n = x2.shape[0]
    info_i, info_f, cnt = _route(logits, t)
    counts = cnt[0, :N_EXPERTS].astype(I32)
    padded = (counts + MOE_TILE - 1) // MOE_TILE * MOE_TILE
    pad_ends = jnp.cumsum(padded)
    pad_starts = pad_ends - padded
    slot0 = pad_starts[info_i[:, 0]] + info_i[:, 2]
    slot1 = pad_starts[info_i[:, 1]] + info_i[:, 3]
    cap = (2 * n // MOE_TILE + N_EXPERTS) * MOE_TILE
    n_tiles = cap // MOE_TILE
    n_used = (pad_ends[-1:] // MOE_TILE).astype(I32)
    tile_e = jnp.minimum(jnp.searchsorted(pad_ends, jnp.arange(n_tiles, dtype=I32) * MOE_TILE, side="right"),
                         N_EXPERTS - 1).astype(I32)
    last_e = tile_e[jnp.maximum(n_used[0] - 1, 0)]
    tile_e = jnp.where(jnp.arange(n_tiles) < n_used[0], tile_e, last_e)
    xs = _dispatch(h2, slot0, slot1, cap, t)
    ys = _moe_ffn(xs, tile_e, n_used, wg, wu, wd)
    return _combine(x2, ys, slot0, slot1, info_f, gt, t)


def _pad_last(a, width):
    return jnp.pad(a, [(0, 0)] * (a.ndim - 1) + [(0, width - a.shape[-1])])


def _swap_rope(a):
    h = MLA_ROPE // 2
    return jnp.concatenate([a[..., :MLA_NOPE], a[..., MLA_NOPE + h:], a[..., MLA_NOPE:MLA_NOPE + h]], axis=-1)


def _layer_params(l, w_in, mla_q_norm_g, mla_w_uq, mla_kv_norm_g, mla_w_ukv, qk_norm_q, qk_norm_k,
                  rwkv_mu, rwkv_w0, rwkv_w_up, rwkv_a0, rwkv_a_up, rwkv_g_up, rwkv_k_k, rwkv_k_a, rwkv_r_k):
    w = w_in[l]
    m0 = CONV_IN
    kr0 = m0 + MLA_Q_LORA + MLA_KV_LORA
    kr = w[:, kr0:kr0 + MLA_ROPE]
    h = MLA_ROPE // 2
    zeros = lambda c: jnp.zeros((D_MODEL, c), F32)
    kr_tile = jnp.concatenate([zeros(MLA_NOPE), kr, zeros(LANES - MLA_QK)], axis=1)
    krs_tile = jnp.concatenate([zeros(MLA_NOPE), kr[:, h:], kr[:, :h], zeros(LANES - MLA_QK)], axis=1)
    w_in_p = jnp.concatenate([w[:, :kr0], kr_tile, krs_tile, w[:, kr0 + MLA_ROPE:]], axis=1).astype(BF16)

    wq = mla_w_uq[l].reshape(MLA_Q_LORA, MLA_HEADS, MLA_QK)
    wkv = mla_w_ukv[l].reshape(MLA_KV_LORA, MLA_HEADS, MLA_NOPE + MLA_V)
    flat = lambda a: _pad_last(a, HEAD_PAD).reshape(a.shape[0], MLA_HEADS * HEAD_PAD).astype(BF16)
    mla = dict(
        gql=mla_q_norm_g[l].reshape(1, -1), gkvl=mla_kv_norm_g[l].reshape(1, -1),
        wq=flat(wq), wqs=flat(_swap_rope(wq)),
        wk=flat(wkv[..., :MLA_NOPE]), wv=flat(wkv[..., MLA_NOPE:]),
        gq=_pad_last(qk_norm_q[l], LANES).reshape(1, LANES),
        gqs=_pad_last(_swap_rope(qk_norm_q[l]), LANES).reshape(1, LANES),
        gk=_pad_last(qk_norm_k[l], LANES).reshape(1, LANES),
        gks=_pad_last(_swap_rope(qk_norm_k[l]), LANES).reshape(1, LANES))

    zl = jnp.zeros((LANES // 2, RWKV_CH), F32)
    hid = jnp.arange(RWKV_CH) // RWKV_HEAD
    rwkv = dict(
        mu=rwkv_mu[l].reshape(1, -1), w0=rwkv_w0[l].reshape(1, -1), a0=rwkv_a0[l].reshape(1, -1),
        wup=jnp.concatenate([rwkv_w_up[l], zl], axis=0).astype(BF16),
        aup=jnp.concatenate([zl, rwkv_a_up[l]], axis=0).astype(BF16),
        gup=rwkv_g_up[l].astype(BF16),
        kk=rwkv_k_k[l].reshape(1, -1), ka=rwkv_k_a[l].reshape(1, -1), rk=rwkv_r_k[l].reshape(1, -1),
        ones=(hid[:, None] == hid[None, :]).astype(BF16))
    return w_in_p, mla, rwkv


def kernel(x, c, positions, ada_w, ada_b, ada_layer_bias, norm1_g, norm2_g, w_in, w_out, conv_w, conv_b, conv_ln_g, conv_ln_b, mla_q_norm_g, mla_w_uq, mla_kv_norm_g, mla_w_ukv, qk_norm_q, qk_norm_k, rwkv_mu, rwkv_w0, rwkv_w_up, rwkv_a0, rwkv_a_up, rwkv_g_up, rwkv_k_k, rwkv_k_a, rwkv_r_k, rwkv_ln_g, rwkv_ln_b, ffn_w_gate, ffn_w_up, ffn_w_down, moe_router, moe_w_gate, moe_w_up, moe_w_down):
    b, t, d = x.shape
    n = b * t
    mods = _cond(c, ada_w, ada_b, ada_layer_bias)
    cos_t, sin_t = _rope_tables(positions)
    x2 = x.reshape(n, d)
    for l in range(DEPTH):
        sh1, sc1, gt1, sh2, sc2, gt2 = [mods[l, :, k * d:(k + 1) * d].reshape(b, 1, d) for k in range(N_MOD)]
        w_in_p, mla_p, rwkv_p = _layer_params(
            l, w_in, mla_q_norm_g, mla_w_uq, mla_kv_norm_g, mla_w_ukv, qk_norm_q, qk_norm_k,
            rwkv_mu, rwkv_w0, rwkv_w_up, rwkv_a0, rwkv_a_up, rwkv_g_up, rwkv_k_k, rwkv_k_a, rwkv_r_k)
        u_conv, u_mla, u_rwkv = _in_proj(x2, sc1, sh1, norm1_g[l].reshape(1, d), w_in_p, t)
        y_conv = _conformer_conv(u_conv.reshape(b, t, CONV_IN), conv_w[l], conv_b[l], conv_ln_g[l], conv_ln_b[l])
        q, kt, v = _mla_prep(u_mla.reshape(b, t, MLA_PAD_IN), cos_t, sin_t, mla_p)
        y_mla = _attention(q, kt, v)
        r_, lw_, k_, v_, a_, b_, bonus, gate = _rwkv_prep(u_rwkv.reshape(b, t, RWKV_IN), rwkv_p)
        qc, y0, g_, f_ = _rwkv_intra(r_, lw_, k_, v_, a_, b_)
        y_rwkv = _rwkv_seq(qc, y0, g_, f_, bonus, gate, rwkv_ln_g[l].reshape(1, -1), rwkv_ln_b[l].reshape(1, -1),
                           rwkv_p["ones"])
        i = l // 2
        router = None
        if l % 2 == 1:
            router = _pad_last(moe_router[i], LANES)
        outs = _out_proj(x2, y_conv.reshape(n, CONV_CH), y_mla.reshape(n, MLA_OUT), y_rwkv.reshape(n, RWKV_CH),
                         w_out[l].astype(BF16), gt1, norm2_g[l].reshape(1, d), sc2, sh2, router, t)
        if l % 2 == 0:
            x2, h2 = outs
            x2 = _dense_ffn(h2, x2, gt2, ffn_w_gate[i].astype(BF16), ffn_w_up[i].astype(BF16),
                            ffn_w_down[i].astype(BF16), t)
        else:
            x2, h2, logits = outs
            x2 = _moe(h2, x2, logits, gt2, moe_w_gate[i].astype(BF16), moe_w_up[i].astype(BF16),
                      moe_w_down[i].astype(BF16), t)
    return x2.reshape(b, t, d)
```

```python
import functools

import jax
import jax.numpy as jnp
from jax import lax
from jax.experimental import pallas as pl
from jax.experimental.pallas import tpu as pltpu

F32 = jnp.float32
BF16 = jnp.bfloat16
I32 = jnp.int32

D_MODEL = 1024
DEPTH = 4
N_MOD = 6
CONV_CH = 256
CONV_K = 31
CONV_IN = 2 * CONV_CH
MLA_NOPE = 64
MLA_ROPE = 32
MLA_V = 64
MLA_QK = MLA_NOPE + MLA_ROPE
MLA_HEADS = 8
MLA_OUT = MLA_HEADS * MLA_V
MLA_Q_LORA = 256
MLA_KV_LORA = 128
MLA_IN = MLA_Q_LORA + MLA_KV_LORA + MLA_ROPE
ROPE_THETA = 10000.0
RWKV_HEAD = 64
RWKV_CH = 256
RWKV_HEADS = 4
RWKV_IN = 1024
N_EXPERTS = 8
RMS_EPS = 1e-6
LN_EPS = 1e-5
RWKV_GN_EPS = 64e-5

LANES = 128
HEAD_PAD = LANES
MLA_PAD_IN = MLA_Q_LORA + MLA_KV_LORA + 2 * LANES
IN_PAD = CONV_IN + MLA_PAD_IN + RWKV_IN
CHUNK = 64
MOE_TILE = 512
VMEM_LIMIT = 56 * 1024 * 1024

_NEG = -1e30


def _cp(n_axes, vmem=None):
    return pltpu.CompilerParams(dimension_semantics=("arbitrary",) * n_axes,
                                vmem_limit_bytes=vmem if vmem else VMEM_LIMIT)


def _mm(a, b):
    return jnp.dot(a.astype(BF16), b.astype(BF16), preferred_element_type=F32)


def _mm_nt(a, b):
    return lax.dot_general(a.astype(BF16), b.astype(BF16), (((1,), (1,)), ((), ())),
                           preferred_element_type=F32)


def _mm_tn(a, b):
    return lax.dot_general(a.astype(BF16), b.astype(BF16), (((0,), (0,)), ((), ())),
                           preferred_element_type=F32)


def _sigmoid(x):
    return 1.0 / (1.0 + jnp.exp(-x))


def _split2(x):
    hi = x.astype(BF16)
    lo = (x - hi.astype(F32)).astype(BF16)
    return hi, lo


def _seg_sum(x, ones_bd):
    hi, lo = _split2(x)
    return (jnp.dot(hi, ones_bd, preferred_element_type=F32)
            + jnp.dot(lo, ones_bd, preferred_element_type=F32))


def _cond_kernel(c_ref, w_ref, b_ref, lb_ref, o_ref):
    c = c_ref[...]
    s = c * _sigmoid(c)
    base = jnp.dot(s, w_ref[...], precision=lax.Precision.HIGHEST, preferred_element_type=F32) + b_ref[...]
    for l in range(DEPTH):
        o_ref[l] = base + lb_ref[l]


def _cond(c, ada_w, ada_b, ada_layer_bias):
    b = c.shape[0]
    n = ada_w.shape[1]
    tn = 1536
    cp = jnp.zeros((8, D_MODEL), F32).at[:b].set(c)
    out = pl.pallas_call(
        _cond_kernel,
        grid=(n // tn,),
        in_specs=[pl.BlockSpec((8, D_MODEL), lambda j: (0, 0)),
                  pl.BlockSpec((D_MODEL, tn), lambda j: (0, j)),
                  pl.BlockSpec((1, tn), lambda j: (0, j)),
                  pl.BlockSpec((DEPTH, 1, tn), lambda j: (0, 0, j))],
        out_specs=pl.BlockSpec((DEPTH, 8, tn), lambda j: (0, 0, j)),
        out_shape=jax.ShapeDtypeStruct((DEPTH, 8, n), F32),
        compiler_params=_cp(1),
        name="cond",
    )(cp, ada_w, ada_b.reshape(1, n), ada_layer_bias.reshape(DEPTH, 1, n))
    return out[:, :b]


def _rope_kernel(pos_ref, invf_ref, c_ref, s_ref):
    pos = pos_ref[0].astype(F32)
    ang = pos * invf_ref[...]
    lane = lax.broadcasted_iota(I32, ang.shape, 1)
    cs = jnp.cos(ang)
    sn = jnp.sin(ang)
    c_ref[0] = jnp.where(lane < MLA_NOPE, 1.0, jnp.where(lane < MLA_QK, cs, 0.0))
    s_ref[0] = jnp.where((lane >= MLA_NOPE) & (lane < MLA_NOPE + 16), -sn,
                         jnp.where((lane >= MLA_NOPE + 16) & (lane < MLA_QK), sn, 0.0))


def _rope_tables(positions):
    b, t = positions.shape
    tt = min(512, t)
    inv_freq = ROPE_THETA ** (-jnp.arange(0, MLA_ROPE, 2, dtype=F32) / MLA_ROPE)
    invf = jnp.zeros((1, LANES), F32)
    invf = invf.at[0, MLA_NOPE:MLA_NOPE + 16].set(inv_freq).at[0, MLA_NOPE + 16:MLA_QK].set(inv_freq)
    spec = pl.BlockSpec((1, tt, LANES), lambda bi, i: (bi, i, 0))
    return pl.pallas_call(
        _rope_kernel,
        grid=(b, t // tt),
        in_specs=[pl.BlockSpec((1, tt, 1), lambda bi, i: (bi, i, 0)),
                  pl.BlockSpec((1, LANES), lambda bi, i: (0, 0))],
        out_specs=[spec, spec],
        out_shape=[jax.ShapeDtypeStruct((b, t, LANES), F32)] * 2,
        compiler_params=_cp(2),
        name="rope_tables",
    )(positions.reshape(b, t, 1), invf)


def _in_proj_kernel(x_ref, sc_ref, sh_ref, g_ref, w_ref, oc_ref, om_ref, or_ref):
    x = x_ref[...]
    ms = jnp.mean(x * x, axis=-1, keepdims=True)
    h = x * lax.rsqrt(ms + RMS_EPS) * g_ref[...]
    h = h * (1.0 + sc_ref[0]) + sh_ref[0]
    hb = h.astype(BF16)
    oc_ref[...] = jnp.dot(hb, w_ref[:, 0:CONV_IN], preferred_element_type=F32)
    om_ref[...] = jnp.dot(hb, w_ref[:, CONV_IN:CONV_IN + MLA_PAD_IN], preferred_element_type=F32)
    or_ref[...] = jnp.dot(hb, w_ref[:, CONV_IN + MLA_PAD_IN:IN_PAD], preferred_element_type=F32)


def _in_proj(x2, sc, sh, g, w, t):
    n = x2.shape[0]
    tm = min(512, t)
    tpb = t // tm
    vec = pl.BlockSpec((1, 1, D_MODEL), lambda i: (i // tpb, 0, 0))
    return pl.pallas_call(
        _in_proj_kernel,
        grid=(n // tm,),
        in_specs=[pl.BlockSpec((tm, D_MODEL), lambda i: (i, 0)), vec, vec,
                  pl.BlockSpec((1, D_MODEL), lambda i: (0, 0)),
                  pl.BlockSpec((D_MODEL, IN_PAD), lambda i: (0, 0))],
        out_specs=[pl.BlockSpec((tm, CONV_IN), lambda i: (i, 0)),
                   pl.BlockSpec((tm, MLA_PAD_IN), lambda i: (i, 0)),
                   pl.BlockSpec((tm, RWKV_IN), lambda i: (i, 0))],
        out_shape=[jax.ShapeDtypeStruct((n, CONV_IN), F32),
                   jax.ShapeDtypeStruct((n, MLA_PAD_IN), F32),
                   jax.ShapeDtypeStruct((n, RWKV_IN), F32)],
        compiler_params=_cp(1),
        name="in_proj",
    )(x2, sc, sh, g, w)


_CONV_HALO = 32


def _conv_kernel(u_ref, up_ref, w_ref, b_ref, g_ref, bb_ref, o_ref, zbuf, *, tt):
    i = pl.program_id(1)
    u = u_ref[0]
    z = u[:, :CONV_CH] * _sigmoid(u[:, CONV_CH:])
    up = up_ref[0]
    zp = up[:, :CONV_CH] * _sigmoid(up[:, CONV_CH:])
    zp = jnp.where(i > 0, zp, 0.0)
    zbuf[0:_CONV_HALO, :] = zp
    zbuf[_CONV_HALO:, :] = z
    acc = jnp.zeros((tt, CONV_CH), F32)
    for k in range(CONV_K):
        acc = acc + zbuf[pl.ds(_CONV_HALO - (CONV_K - 1) + k, tt), :] * w_ref[k:k + 1, :]
    y = acc + b_ref[...]
    mu = jnp.mean(y, axis=-1, keepdims=True)
    yc = y - mu
    var = jnp.mean(yc * yc, axis=-1, keepdims=True)
    yn = yc * lax.rsqrt(var + LN_EPS) * g_ref[...] + bb_ref[...]
    o_ref[0] = yn * _sigmoid(yn)


def _conformer_conv(u3, conv_w, conv_b, ln_g, ln_b):
    b, t, _ = u3.shape
    tt = min(512, t)
    hb = tt // _CONV_HALO
    vec = pl.BlockSpec((1, CONV_CH), lambda bi, i: (0, 0))
    return pl.pallas_call(
        functools.partial(_conv_kernel, tt=tt),
        grid=(b, t // tt),
        in_specs=[pl.BlockSpec((1, tt, CONV_IN), lambda bi, i: (bi, i, 0)),
                  pl.BlockSpec((1, _CONV_HALO, CONV_IN), lambda bi, i: (bi, jnp.maximum(i * hb - 1, 0), 0)),
                  pl.BlockSpec((CONV_K, CONV_CH), lambda bi, i: (0, 0)), vec, vec, vec],
        out_specs=pl.BlockSpec((1, tt, CONV_CH), lambda bi, i: (bi, i, 0)),
        out_shape=jax.ShapeDtypeStruct((b, t, CONV_CH), F32),
        scratch_shapes=[pltpu.VMEM((tt + _CONV_HALO, CONV_CH), F32)],
        compiler_params=_cp(2),
        name="conformer_conv",
    )(u3, u3, conv_w, conv_b.reshape(1, -1), ln_g.reshape(1, -1), ln_b.reshape(1, -1))


def _mla_prep_kernel(um_ref, c_ref, s_ref, gql_ref, wq_ref, wqs_ref, gkvl_ref, wk_ref, wv_ref,
                     gq_ref, gqs_ref, gk_ref, gks_ref, q_ref, kt_ref, v_ref):
    um = um_ref[0]
    cq = um[:, 0:MLA_Q_LORA]
    ckv = um[:, MLA_Q_LORA:MLA_Q_LORA + MLA_KV_LORA]
    kr = um[:, MLA_Q_LORA + MLA_KV_LORA:MLA_Q_LORA + MLA_KV_LORA + LANES]
    krs = um[:, MLA_Q_LORA + MLA_KV_LORA + LANES:MLA_PAD_IN]
    cqn = (cq * lax.rsqrt(jnp.mean(cq * cq, axis=-1, keepdims=True) + RMS_EPS) * gql_ref[...]).astype(BF16)
    ckvn = (ckv * lax.rsqrt(jnp.mean(ckv * ckv, axis=-1, keepdims=True) + RMS_EPS) * gkvl_ref[...]).astype(BF16)
    q = jnp.dot(cqn, wq_ref[...], preferred_element_type=F32)
    qs = jnp.dot(cqn, wqs_ref[...], preferred_element_type=F32)
    kn = jnp.dot(ckvn, wk_ref[...], preferred_element_type=F32)
    vv = jnp.dot(ckvn, wv_ref[...], preferred_element_type=F32)
    cos_t = c_ref[0]
    sin_t = s_ref[0]
    lane = lax.broadcasted_iota(I32, cos_t.shape, 1)
    scale = float(MLA_QK) ** -0.5
    inv_d = 1.0 / MLA_QK
    for h in range(MLA_HEADS):
        sl = slice(h * HEAD_PAD, (h + 1) * HEAD_PAD)
        qh = q[:, sl]
        r = lax.rsqrt(jnp.sum(qh * qh, axis=-1, keepdims=True) * inv_d + RMS_EPS)
        qf = qh * (r * gq_ref[...]) * cos_t + qs[:, sl] * (r * gqs_ref[...]) * sin_t
        q_ref[0, :, sl] = (qf * scale).astype(BF16)
        kh = kn[:, sl] + kr
        r = lax.rsqrt(jnp.sum(kh * kh, axis=-1, keepdims=True) * inv_d + RMS_EPS)
        kf = kh * (r * gk_ref[...]) * cos_t + krs * (r * gks_ref[...]) * sin_t
        kt_ref[0, h] = kf.T.astype(BF16)
        v_ref[0, h] = jnp.where(lane == MLA_V, 1.0, vv[:, sl]).astype(BF16)


def _mla_prep(um3, cos_t, sin_t, p):
    b, t, _ = um3.shape
    tt = min(256, t)
    full = lambda shape: pl.BlockSpec(shape, lambda bi, i: (0,) * len(shape))
    tab = pl.BlockSpec((1, tt, LANES), lambda bi, i: (bi, i, 0))
    return pl.pallas_call(
        _mla_prep_kernel,
        grid=(b, t // tt),
        in_specs=[pl.BlockSpec((1, tt, MLA_PAD_IN), lambda bi, i: (bi, i, 0)), tab, tab,
                  full((1, MLA_Q_LORA)), full((MLA_Q_LORA, MLA_HEADS * HEAD_PAD)),
                  full((MLA_Q_LORA, MLA_HEADS * HEAD_PAD)),
                  full((1, MLA_KV_LORA)), full((MLA_KV_LORA, MLA_HEADS * HEAD_PAD)),
                  full((MLA_KV_LORA, MLA_HEADS * HEAD_PAD)),
                  full((1, LANES)), full((1, LANES)), full((1, LANES)), full((1, LANES))],
        out_specs=[pl.BlockSpec((1, tt, MLA_HEADS * HEAD_PAD), lambda bi, i: (bi, i, 0)),
                   pl.BlockSpec((1, MLA_HEADS, HEAD_PAD, tt), lambda bi, i: (bi, 0, 0, i)),
                   pl.BlockSpec((1, MLA_HEADS, tt, HEAD_PAD), lambda bi, i: (bi, 0, i, 0))],
        out_shape=[jax.ShapeDtypeStruct((b, t, MLA_HEADS * HEAD_PAD), BF16),
                   jax.ShapeDtypeStruct((b, MLA_HEADS, HEAD_PAD, t), BF16),
                   jax.ShapeDtypeStruct((b, MLA_HEADS, t, HEAD_PAD), BF16)],
        compiler_params=_cp(2),
        name="mla_prep",
    )(um3, cos_t, sin_t, p["gql"], p["wq"], p["wqs"], p["gkvl"], p["wk"], p["wv"],
      p["gq"], p["gqs"], p["gk"], p["gks"])


_ATTN_HEAD_GROUP = 4
def _attn_kernel(q_ref, kt_ref, v_ref, o_ref, *, tq, tk):
    i = pl.program_id(1)
    nfull = i * (tq // tk)
    row = lax.broadcasted_iota(I32, (tq, tk), 0)
    col = lax.broadcasted_iota(I32, (tq, tk), 1)
    outs = []

    def step(h, off, m, acc, mask):
        q = q_ref[0, :, h * HEAD_PAD:(h + 1) * HEAD_PAD]
        s = jnp.dot(q, kt_ref[0, h, :, pl.ds(off, tk)], preferred_element_type=F32)
        if mask is not None:
            s = jnp.where(mask, s, _NEG)
        m_new = jnp.maximum(m, jnp.max(s, axis=-1, keepdims=True))
        p = jnp.exp(s - m_new)
        acc = acc * jnp.exp(m - m_new) + jnp.dot(p.astype(BF16), v_ref[0, h, pl.ds(off, tk), :],
                                                 preferred_element_type=F32)
        return m_new, acc

    for h0 in range(0, MLA_HEADS, _ATTN_HEAD_GROUP):
        heads = range(h0, h0 + _ATTN_HEAD_GROUP)

        def body(kb, carry):
            off = pl.multiple_of(kb * tk, tk)
            return tuple(step(h, off, carry[n][0], carry[n][1], None) for n, h in enumerate(heads))

        init = tuple((jnp.full((tq, 1), _NEG, F32), jnp.zeros((tq, HEAD_PAD), F32)) for _ in heads)
        carry = lax.fori_loop(0, nfull, body, init)
        for d in range(tq // tk):
            off = pl.multiple_of(i * tq + d * tk, tk)
            carry = tuple(step(h, off, carry[n][0], carry[n][1], col + d * tk <= row)
                          for n, h in enumerate(heads))
        for m, acc in carry:
            outs.append(acc[:, :MLA_V] / acc[:, MLA_V:MLA_V + 1])
    o_ref[0] = jnp.concatenate(outs, axis=-1)


def _attention(q, kt, v):
    b, t, _ = q.shape
    tq = min(512, t)
    tk = min(256, t)
    return pl.pallas_call(
        functools.partial(_attn_kernel, tq=tq, tk=tk),
        grid=(b, t // tq),
        in_specs=[pl.BlockSpec((1, tq, MLA_HEADS * HEAD_PAD), lambda bi, i: (bi, i, 0)),
                  pl.BlockSpec((1, MLA_HEADS, HEAD_PAD, t), lambda bi, i: (bi, 0, 0, 0)),
                  pl.BlockSpec((1, MLA_HEADS, t, HEAD_PAD), lambda bi, i: (bi, 0, 0, 0))],
        out_specs=pl.BlockSpec((1, tq, MLA_OUT), lambda bi, i: (bi, i, 0)),
        out_shape=jax.ShapeDtypeStruct((b, t, MLA_OUT), F32),
        compiler_params=_cp(2),
        name="mla_attention",
    )(q, kt, v)


def _rwkv_prep_kernel(u_ref, up_ref, mu_ref, w0_ref, wup_ref, a0_ref, aup_ref, gup_ref, kk_ref, ka_ref,
                      rk_ref, ones_ref, r_o, lw_o, k_o, v_o, a_o, b_o, bonus_o, g_o):
    i = pl.program_id(1)
    u = u_ref[0]
    prow = jnp.where(i > 0, up_ref[0][7:8, :], 0.0)
    rowi = lax.broadcasted_iota(I32, u.shape, 0)
    prev = jnp.where(rowi == 0, prow, pltpu.roll(u, 1, axis=0))
    uf = u + (prev - u) * mu_ref[...]
    c = RWKV_CH
    r = uf[:, 0:c]
    k = uf[:, c:2 * c]
    v = uf[:, 2 * c:3 * c]
    wa = uf[:, 3 * c:3 * c + LANES]
    gd = uf[:, 3 * c + LANES:]
    ones_bd = ones_ref[...]
    wl = w0_ref[...] + _mm(jnp.tanh(wa), wup_ref[...])
    z = -wl
    softplus = jnp.maximum(z, 0.0) + jnp.log(1.0 + jnp.exp(-jnp.abs(z)))
    lw = -jnp.exp(-softplus - 0.5)
    a_s = _sigmoid(a0_ref[...] + _mm(wa, aup_ref[...]))
    g = _mm(_sigmoid(gd), gup_ref[...])
    kk = k * kk_ref[...]
    nrm = jnp.maximum(jnp.sqrt(_seg_sum(kk * kk, ones_bd)), 1e-12)
    kkn = kk / nrm
    k2 = k * (1.0 + (a_s - 1.0) * ka_ref[...])
    r_o[0] = r
    lw_o[0] = lw
    k_o[0] = k2
    v_o[0] = v
    a_o[0] = -kkn
    b_o[0] = kkn * a_s
    bonus_o[0] = _seg_sum(r * k2 * rk_ref[...], ones_bd) * v
    g_o[0] = g


def _rwkv_prep(u3, p):
    b, t, _ = u3.shape
    tt = min(512, t)
    hb = tt // 8
    full = lambda shape: pl.BlockSpec(shape, lambda bi, i: (0,) * len(shape))
    out = pl.BlockSpec((1, tt, RWKV_CH), lambda bi, i: (bi, i, 0))
    return pl.pallas_call(
        _rwkv_prep_kernel,
        grid=(b, t // tt),
        in_specs=[pl.BlockSpec((1, tt, RWKV_IN), lambda bi, i: (bi, i, 0)),
                  pl.BlockSpec((1, 8, RWKV_IN), lambda bi, i: (bi, jnp.maximum(i * hb - 1, 0), 0)),
                  full((1, RWKV_IN)), full((1, RWKV_CH)), full((LANES, RWKV_CH)), full((1, RWKV_CH)),
                  full((LANES, RWKV_CH)), full((LANES, RWKV_CH)), full((1, RWKV_CH)), full((1, RWKV_CH)),
                  full((1, RWKV_CH)), full((RWKV_CH, RWKV_CH))],
        out_specs=[out] * 8,
        out_shape=[jax.ShapeDtypeStruct((b, t, RWKV_CH), F32)] * 8,
        compiler_params=_cp(2),
        name="rwkv_prep",
    )(u3, u3, p["mu"], p["w0"], p["wup"], p["a0"], p["aup"], p["gup"], p["kk"], p["ka"], p["rk"], p["ones"])


def _intra_chunks(rs, lws, ks, vs, as_, bs, mask_bd, eye_bd, tril, strict, incl, eye_p):
    c = CHUNK
    each = lambda fn, *lists: [fn(*xs) for xs in zip(*lists)]

    def bd(x):
        xb = x.astype(BF16)
        return jnp.where(mask_bd, jnp.concatenate([xb, xb, xb, xb], axis=0), jnp.zeros((), BF16))

    def prefix(lw):
        hi = lw.astype(BF16)
        rem = lw - hi.astype(F32)
        mid = rem.astype(BF16)
        lo = (rem - mid.astype(F32)).astype(BF16)
        return (jnp.dot(tril, hi, preferred_element_type=F32) + jnp.dot(tril, mid, preferred_element_type=F32)
                + jnp.dot(tril, lo, preferred_element_type=F32))

    cums = each(prefix, lws)
    tots = each(lambda cum: cum[c - 1:c, :], cums)
    e_negs = each(lambda cum: jnp.exp(-cum), cums)
    e_ends = each(lambda tot, cum: jnp.exp(tot - cum), tots, cums)
    rts = each(lambda r, cum: r * jnp.exp(cum), rs, cums)
    ats = each(lambda a, cum, lw: a * jnp.exp(cum - lw), as_, cums, lws)
    stacks = each(lambda at, rt: jnp.concatenate([at, rt], axis=0), ats, rts)
    s1s = each(lambda st, b, e: _mm_nt(st, bd(b * e)), stacks, bs, e_negs)
    s2s = each(lambda st, k, e: _mm_nt(st, bd(k * e)), stacks, ks, e_negs)
    a_abs = each(lambda s1: jnp.where(strict, s1[:c], 0.0), s1s)
    m_rbs = each(lambda s1: jnp.where(incl, s1[c:], 0.0), s1s)
    a_aks = each(lambda s2: jnp.where(strict, s2[:c], 0.0), s2s)
    m_rks = each(lambda s2: jnp.where(incl, s2[c:], 0.0), s2s)
    xs = each(lambda a_ab: eye_p + a_ab, a_abs)
    aks = each(lambda a_ab: _mm(a_ab, bd(a_ab)), a_abs)
    for j in range(5):
        if j < 4:
            pps = each(lambda ak, x: _mm(jnp.concatenate([ak, x], axis=0), bd(ak)), aks, xs)
            xs = each(lambda x, pp: x + pp[c:], xs, pps)
            aks = each(lambda pp: pp[:c], pps)
        else:
            xs = each(lambda x, ak: x + _mm(x, bd(ak)), xs, aks)
    bdvs = each(bd, vs)
    avs = each(_mm, a_aks, bdvs)
    ahs = each(lambda x, at: _mm(x, bd(at)), xs, ats)
    uhs = each(lambda x, av: _mm(x, bd(av)), xs, avs)
    qs = each(lambda rt, m_rb, ah: rt + _mm(m_rb, bd(ah)), rts, m_rbs, ahs)
    y0s = each(lambda m_rb, uh, m_rk, bdv: _mm(m_rb, bd(uh)) + _mm(m_rk, bdv), m_rbs, uhs, m_rks, bdvs)
    gs = each(lambda b, e, ah, tot: jnp.where(mask_bd, _mm_tn(b * e, ah), 0.0) + jnp.where(eye_bd, jnp.exp(tot), 0.0),
              bs, e_ends, ahs, tots)
    fs = each(lambda b, k, e, uh, v: jnp.where(mask_bd, _mm_tn(b * e, uh) + _mm_tn(k * e, v), 0.0),
              bs, ks, e_ends, uhs, vs)
    return qs, y0s, gs, fs


def _rwkv_intra_kernel(r_ref, lw_ref, k_ref, v_ref, a_ref, b_ref, q_o, y0_o, g_o, f_o, *, cps):
    c = CHUNK
    n = RWKV_CH
    ri = lax.broadcasted_iota(I32, (n, n), 0)
    ci = lax.broadcasted_iota(I32, (n, n), 1)
    mask_bd = (ri // RWKV_HEAD) == (ci // RWKV_HEAD)
    eye_bd = ri == ci
    t64 = lax.broadcasted_iota(I32, (c, c), 0)
    i64 = lax.broadcasted_iota(I32, (c, c), 1)
    tril = jnp.where(i64 <= t64, 1.0, 0.0).astype(BF16)
    tp = lax.broadcasted_iota(I32, (c, n), 0)
    ip = lax.broadcasted_iota(I32, (c, n), 1) & (RWKV_HEAD - 1)
    strict = ip < tp
    incl = ip <= tp
    eye_p = jnp.where(ip == tp, 1.0, 0.0)
    sls = [slice(s * c, (s + 1) * c) for s in range(cps)]
    load = lambda ref: [ref[0, sl, :] for sl in sls]
    qs, y0s, gs, fs = _intra_chunks(load(r_ref), load(lw_ref), load(k_ref), load(v_ref), load(a_ref), load(b_ref),
                                    mask_bd, eye_bd, tril, strict, incl, eye_p)
    for s, sl in enumerate(sls):
        q_o[0, sl, :] = qs[s]
        y0_o[0, sl, :] = y0s[s]
        g_o[0, s] = gs[s]
        f_o[0, s] = fs[s]


def _rwkv_intra(r, lw, k, v, a, b_):
    b, t, n = r.shape
    nc = t // CHUNK
    cps = max(d for d in (1, 2, 4, 8) if nc % d == 0)
    tok = pl.BlockSpec((1, cps * CHUNK, n), lambda bi, i: (bi, i, 0))
    mat = pl.BlockSpec((1, cps, n, n), lambda bi, i: (bi, i, 0, 0))
    return pl.pallas_call(
        functools.partial(_rwkv_intra_kernel, cps=cps),
        grid=(b, nc // cps),
        in_specs=[tok] * 6,
        out_specs=[tok, tok, mat, mat],
        out_shape=[jax.ShapeDtypeStruct((b, t, n), F32), jax.ShapeDtypeStruct((b, t, n), F32),
                   jax.ShapeDtypeStruct((b, nc, n, n), F32), jax.ShapeDtypeStruct((b, nc, n, n), F32)],
        compiler_params=_cp(2),
        name="rwkv_intra",
    )(r, lw, k, v, a, b_)


def _rwkv_seq_kernel(q_ref, y0_ref, g_ref, f_ref, bonus_ref, gate_ref, lng_ref, lnb_ref, ones_ref, o_ref, h_ref,
                     *, nb):
    c = pl.program_id(0)

    @pl.when(c == 0)
    def _():
        h_ref[...] = jnp.zeros_like(h_ref)

    ones_bd = ones_ref[...]
    inv = 1.0 / RWKV_HEAD
    for b in range(nb):
        hb = h_ref[b].astype(BF16)
        y = jnp.dot(q_ref[b].astype(BF16), hb, preferred_element_type=F32) + y0_ref[b]
        h_ref[b] = jnp.dot(g_ref[b, 0].astype(BF16), hb, preferred_element_type=F32) + f_ref[b, 0]
        mu = _seg_sum(y, ones_bd) * inv
        yc = y - mu
        var = _seg_sum(yc * yc, ones_bd) * inv
        yn = yc * lax.rsqrt(var + RWKV_GN_EPS) * lng_ref[...] + lnb_ref[...]
        o_ref[b] = (yn + bonus_ref[b]) * gate_ref[b]


def _rwkv_seq(q, y0, g, f, bonus, gate, ln_g, ln_b, ones_bd):
    b, t, n = q.shape
    nc = t // CHUNK
    tok = pl.BlockSpec((b, CHUNK, n), lambda c: (0, c, 0))
    mat = pl.BlockSpec((b, 1, n, n), lambda c: (0, c, 0, 0))
    vec = pl.BlockSpec((1, n), lambda c: (0, 0))
    return pl.pallas_call(
        functools.partial(_rwkv_seq_kernel, nb=b),
        grid=(nc,),
        in_specs=[tok, tok, mat, mat, tok, tok, vec, vec, pl.BlockSpec((n, n), lambda c: (0, 0))],
        out_specs=tok,
        out_shape=jax.ShapeDtypeStruct((b, t, n), F32),
        scratch_shapes=[pltpu.VMEM((b, n, n), F32)],
        compiler_params=_cp(1),
        name="rwkv_seq",
    )(q, y0, g, f, bonus, gate, ln_g, ln_b, ones_bd)


def _out_proj_kernel(x_ref, yc_ref, ym_ref, yr_ref, w_ref, gt_ref, g2_ref, sc_ref, sh_ref, *rest, with_router):
    if with_router:
        wr_ref, xo_ref, h_ref, lg_ref = rest
    else:
        xo_ref, h_ref = rest
    mixed = (_mm(yc_ref[...], w_ref[0:CONV_CH, :])
             + _mm(ym_ref[...], w_ref[CONV_CH:CONV_CH + MLA_OUT, :])
             + _mm(yr_ref[...], w_ref[CONV_CH + MLA_OUT:, :]))
    x = x_ref[...] + gt_ref[0] * mixed
    xo_ref[...] = x
    ms = jnp.mean(x * x, axis=-1, keepdims=True)
    h = x * lax.rsqrt(ms + RMS_EPS) * g2_ref[...]
    h = h * (1.0 + sc_ref[0]) + sh_ref[0]
    h_ref[...] = h
    if with_router:
        h_hi, h_lo = _split2(h)
        w_hi, w_lo = _split2(wr_ref[...])
        lg_ref[...] = (jnp.dot(h_hi, w_hi, preferred_element_type=F32)
                       + jnp.dot(h_lo, w_hi, preferred_element_type=F32)
                       + jnp.dot(h_hi, w_lo, preferred_element_type=F32))


def _out_proj(x2, yc, ym, yr, w_out, gt, g2, sc, sh, router, t):
    n = x2.shape[0]
    tm = min(512, t)
    tpb = t // tm
    vec = pl.BlockSpec((1, 1, D_MODEL), lambda i: (i // tpb, 0, 0))
    row = lambda w: pl.BlockSpec((tm, w), lambda i: (i, 0))
    in_specs = [row(D_MODEL), row(CONV_CH), row(MLA_OUT), row(RWKV_CH),
                pl.BlockSpec((D_MODEL, D_MODEL), lambda i: (0, 0)), vec,
                pl.BlockSpec((1, D_MODEL), lambda i: (0, 0)), vec, vec]
    out_specs = [row(D_MODEL), row(D_MODEL)]
    out_shape = [jax.ShapeDtypeStruct((n, D_MODEL), F32), jax.ShapeDtypeStruct((n, D_MODEL), F32)]
    args = [x2, yc, ym, yr, w_out, gt, g2, sc, sh]
    if router is not None:
        in_specs.append(pl.BlockSpec((D_MODEL, LANES), lambda i: (0, 0)))
        out_specs.append(row(LANES))
        out_shape.append(jax.ShapeDtypeStruct((n, LANES), F32))
        args.append(router)
    return pl.pallas_call(
        functools.partial(_out_proj_kernel, with_router=router is not None),
        grid=(n // tm,),
        in_specs=in_specs, out_specs=out_specs, out_shape=out_shape,
        compiler_params=_cp(1),
        name="out_proj",
    )(*args)


def _swiglu_step(h_ref, wg_ref, wu_ref, wd_ref, hb_ref, acc_ref, j):
    @pl.when(j == 0)
    def _():
        hb_ref[...] = h_ref[...].astype(BF16)
        acc_ref[...] = jnp.zeros_like(acc_ref)

    hb = hb_ref[...]
    gate = jnp.dot(hb, wg_ref[...], preferred_element_type=F32)
    up = jnp.dot(hb, wu_ref[...], preferred_element_type=F32)
    act = (gate * _sigmoid(gate) * up).astype(BF16)
    acc_ref[...] += jnp.dot(act, wd_ref[...], preferred_element_type=F32)


def _ffn_kernel(h_ref, x_ref, gt_ref, wg_ref, wu_ref, wd_ref, o_ref, hb_ref, acc_ref, *, nj):
    j = pl.program_id(1)
    _swiglu_step(h_ref, wg_ref, wu_ref, wd_ref, hb_ref, acc_ref, j)

    @pl.when(j == nj - 1)
    def _():
        o_ref[...] = x_ref[...] + gt_ref[0] * acc_ref[...]


def _dense_ffn(h2, x2, gt, wg, wu, wd, t):
    n = x2.shape[0]
    ff = wg.shape[1]
    tm = min(512, t)
    tpb = t // tm
    nj = 2
    tf = ff // nj
    row = pl.BlockSpec((tm, D_MODEL), lambda i, j: (i, 0))
    return pl.pallas_call(
        functools.partial(_ffn_kernel, nj=nj),
        grid=(n // tm, nj),
        in_specs=[row, row, pl.BlockSpec((1, 1, D_MODEL), lambda i, j: (i // tpb, 0, 0)),
                  pl.BlockSpec((D_MODEL, tf), lambda i, j: (0, j)),
                  pl.BlockSpec((D_MODEL, tf), lambda i, j: (0, j)),
                  pl.BlockSpec((tf, D_MODEL), lambda i, j: (j, 0))],
        out_specs=row,
        out_shape=jax.ShapeDtypeStruct((n, D_MODEL), F32),
        scratch_shapes=[pltpu.VMEM((tm, D_MODEL), BF16), pltpu.VMEM((tm, D_MODEL), F32)],
        compiler_params=_cp(2),
        name="dense_ffn",
    )(h2, x2, gt, wg, wu, wd)


def _moe_ffn_kernel(te_ref, nu_ref, xs_ref, wg_ref, wu_ref, wd_ref, o_ref, hb_ref, acc_ref, *, nj):
    i = pl.program_id(0)
    j = pl.program_id(1)
    used = i < nu_ref[0]

    @pl.when(used)
    def _():
        _swiglu_step(xs_ref, wg_ref.at[0], wu_ref.at[0], wd_ref.at[0], hb_ref, acc_ref, j)

    @pl.when(j == nj - 1)
    def _():
        o_ref[...] = jnp.where(used, acc_ref[...], 0.0)


def _moe_ffn(xs, tile_e, n_used, wg, wu, wd):
    cap = xs.shape[0]
    ff = wg.shape[2]
    nj = 2
    tf = ff // nj
    tm = MOE_TILE

    def wj(i, j, nu):
        return jnp.where(i < nu[0], j, nj - 1)

    row = pl.BlockSpec((tm, D_MODEL), lambda i, j, te, nu: (i, 0))
    grid_spec = pltpu.PrefetchScalarGridSpec(
        num_scalar_prefetch=2,
        grid=(cap // tm, nj),
        in_specs=[row,
                  pl.BlockSpec((1, D_MODEL, tf), lambda i, j, te, nu: (te[i], 0, wj(i, j, nu))),
                  pl.BlockSpec((1, D_MODEL, tf), lambda i, j, te, nu: (te[i], 0, wj(i, j, nu))),
                  pl.BlockSpec((1, tf, D_MODEL), lambda i, j, te, nu: (te[i], wj(i, j, nu), 0))],
        out_specs=row,
        scratch_shapes=[pltpu.VMEM((tm, D_MODEL), BF16), pltpu.VMEM((tm, D_MODEL), F32)])
    return pl.pallas_call(
        functools.partial(_moe_ffn_kernel, nj=nj),
        grid_spec=grid_spec,
        out_shape=jax.ShapeDtypeStruct((cap, D_MODEL), F32),
        compiler_params=_cp(2),
        name="moe_ffn",
    )(tile_e, n_used, xs, wg, wu, wd)


def _route_kernel(lg_ref, oi_ref, of_ref, cnt_ref, carry_ref, *, tm):
    i = pl.program_id(0)

    @pl.when(i == 0)
    def _():
        carry_ref[...] = jnp.zeros_like(carry_ref)

    lane = lax.broadcasted_iota(I32, (tm, LANES), 1)
    real = lane < N_EXPERTS
    lg = jnp.where(real, lg_ref[...], _NEG)
    ex = jnp.exp(lg - jnp.max(lg, axis=-1, keepdims=True))
    p = ex / jnp.sum(ex, axis=-1, keepdims=True)
    p = jnp.where(real, p, -1.0)
    m1 = jnp.max(p, axis=-1, keepdims=True)
    i1 = jnp.min(jnp.where(p == m1, lane, LANES), axis=-1, keepdims=True)
    p2 = jnp.where(lane == i1, -1.0, p)
    m2 = jnp.max(p2, axis=-1, keepdims=True)
    i2 = jnp.min(jnp.where(p2 == m2, lane, LANES), axis=-1, keepdims=True)
    den = m1 + m2
    oh1 = lane == i1
    oh2 = lane == i2
    oh = jnp.where(oh1 | oh2, 1.0, 0.0)
    rr = lax.broadcasted_iota(I32, (tm, tm), 0)
    cc = lax.broadcasted_iota(I32, (tm, tm), 1)
    lstrict = jnp.where(cc < rr, 1.0, 0.0).astype(BF16)
    pre = jnp.dot(lstrict, oh.astype(BF16), preferred_element_type=F32) + carry_ref[...]
    rank1 = jnp.sum(jnp.where(oh1, pre, 0.0), axis=-1, keepdims=True).astype(I32)
    rank2 = jnp.sum(jnp.where(oh2, pre, 0.0), axis=-1, keepdims=True).astype(I32)
    carry_ref[...] = carry_ref[...] + jnp.sum(oh, axis=0, keepdims=True)
    cnt_ref[...] = carry_ref[...]
    oi_ref[...] = jnp.where(lane == 0, i1, jnp.where(lane == 1, i2, jnp.where(lane == 2, rank1,
                                                                                jnp.where(lane == 3, rank2, 0))))
    of_ref[...] = jnp.where(lane == 0, m1 / den, jnp.where(lane == 1, m2 / den, 0.0))


def _route(logits, t):
    n = logits.shape[0]
    tm = min(512, t)
    row = pl.BlockSpec((tm, LANES), lambda i: (i, 0))
    return pl.pallas_call(
        functools.partial(_route_kernel, tm=tm),
        grid=(n // tm,),
        in_specs=[row],
        out_specs=[row, row, pl.BlockSpec((1, LANES), lambda i: (0, 0))],
        out_shape=[jax.ShapeDtypeStruct((n, LANES), I32), jax.ShapeDtypeStruct((n, LANES), F32),
                   jax.ShapeDtypeStruct((1, LANES), F32)],
        scratch_shapes=[pltpu.VMEM((1, LANES), F32)],
        compiler_params=_cp(1),
        name="moe_route",
    )(logits)


def _dispatch_kernel(s0_ref, s1_ref, h_ref, init_ref, xs_ref, sem, *, tm):
    del init_ref
    base = pl.program_id(0) * tm

    def row_copy(r, slot):
        return pltpu.make_async_copy(h_ref.at[pl.ds(r, 1)], xs_ref.at[pl.ds(slot, 1)], sem)

    def start(r, carry):
        row_copy(r, s0_ref[base + r]).start(priority=0)
        row_copy(r, s1_ref[base + r]).start(priority=1)
        return carry

    lax.fori_loop(0, tm, start, 0)

    def wait(r, carry):
        row_copy(r, s0_ref[base + r]).wait()
        row_copy(r, s1_ref[base + r]).wait()
        return carry

    lax.fori_loop(0, tm, wait, 0)


def _dispatch(h2, slot0, slot1, cap, t):
    n = h2.shape[0]
    tm = min(512, t)
    grid_spec = pltpu.PrefetchScalarGridSpec(
        num_scalar_prefetch=2,
        grid=(n // tm,),
        in_specs=[pl.BlockSpec((tm, D_MODEL), lambda i, s0, s1: (i, 0)),
                  pl.BlockSpec(memory_space=pl.ANY)],
        out_specs=pl.BlockSpec(memory_space=pl.ANY),
        scratch_shapes=[pltpu.SemaphoreType.DMA(())])
    return pl.pallas_call(
        functools.partial(_dispatch_kernel, tm=tm),
        grid_spec=grid_spec,
        out_shape=jax.ShapeDtypeStruct((cap, D_MODEL), F32),
        input_output_aliases={3: 0},
        compiler_params=_cp(1),
        name="moe_dispatch",
    )(slot0, slot1, h2, jnp.zeros((cap, D_MODEL), F32))


def _combine_kernel(s0_ref, s1_ref, x_ref, wf_ref, gt_ref, ys_ref, o_ref, buf0, buf1, sem, *, tm):
    base = pl.program_id(0) * tm

    def row_copy(slot, buf, r):
        return pltpu.make_async_copy(ys_ref.at[pl.ds(slot, 1)], buf.at[pl.ds(r, 1)], sem)

    def start(r, carry):
        row_copy(s0_ref[base + r], buf0, r).start(priority=0)
        row_copy(s1_ref[base + r], buf1, r).start(priority=1)
        return carry

    lax.fori_loop(0, tm, start, 0)

    def wait(r, carry):
        row_copy(s0_ref[base + r], buf0, r).wait()
        row_copy(s1_ref[base + r], buf1, r).wait()
        return carry

    lax.fori_loop(0, tm, wait, 0)
    wf = wf_ref[...]
    f = wf[:, 0:1] * buf0[...] + wf[:, 1:2] * buf1[...]
    o_ref[...] = x_ref[...] + gt_ref[0] * f


def _combine(x2, ys, slot0, slot1, wf, gt, t):
    n = x2.shape[0]
    tm = min(512, t)
    tpb = t // tm
    row = pl.BlockSpec((tm, D_MODEL), lambda i, s0, s1: (i, 0))
    grid_spec = pltpu.PrefetchScalarGridSpec(
        num_scalar_prefetch=2,
        grid=(n // tm,),
        in_specs=[row, pl.BlockSpec((tm, LANES), lambda i, s0, s1: (i, 0)),
                  pl.BlockSpec((1, 1, D_MODEL), lambda i, s0, s1: (i // tpb, 0, 0)),
                  pl.BlockSpec(memory_space=pl.ANY)],
        out_specs=row,
        scratch_shapes=[pltpu.VMEM((tm, D_MODEL), F32), pltpu.VMEM((tm, D_MODEL), F32),
                        pltpu.SemaphoreType.DMA(())])
    return pl.pallas_call(
        functools.partial(_combine_kernel, tm=tm),
        grid_spec=grid_spec,
        out_shape=jax.ShapeDtypeStruct((n, D_MODEL), F32),
        compiler_params=_cp(1),
        name="moe_combine",
    )(slot0, slot1, x2, wf, gt, ys)


def _moe(h2, x2, logits, gt, wg, wu, wd, t):
    n = x2.shape[0]
    info_i, info_f, cnt = _route(logits, t)
    counts = cnt[0, :N_EXPERTS].astype(I32)
    padded = (counts + MOE_TILE - 1) // MOE_TILE * MOE_TILE
    pad_ends = jnp.cumsum(padded)
    pad_starts = pad_ends - padded
    slot0 = pad_starts[info_i[:, 0]] + info_i[:, 2]
    slot1 = pad_starts[info_i[:, 1]] + info_i[:, 3]
    cap = (2 * n // MOE_TILE + N_EXPERTS) * MOE_TILE
    n_tiles = cap // MOE_TILE
    n_used = (pad_ends[-1:] // MOE_TILE).astype(I32)
    tile_e = jnp.minimum(jnp.searchsorted(pad_ends, jnp.arange(n_tiles, dtype=I32) * MOE_TILE, side="right"),
                         N_EXPERTS - 1).astype(I32)
    last_e = tile_e[jnp.maximum(n_used[0] - 1, 0)]
    tile_e = jnp.where(jnp.arange(n_tiles) < n_used[0], tile_e, last_e)
    xs = _dispatch(h2, slot0, slot1, cap, t)
    ys = _moe_ffn(xs, tile_e, n_used, wg, wu, wd)
    return _combine(x2, ys, slot0, slot1, info_f, gt, t)


def _pad_last(a, width):
    return jnp.pad(a, [(0, 0)] * (a.ndim - 1) + [(0, width - a.shape[-1])])


def _swap_rope(a):
    h = MLA_ROPE // 2
    return jnp.concatenate([a[..., :MLA_NOPE], a[..., MLA_NOPE + h:], a[..., MLA_NOPE:MLA_NOPE + h]], axis=-1)


def _layer_params(l, w_in, mla_q_norm_g, mla_w_uq, mla_kv_norm_g, mla_w_ukv, qk_norm_q, qk_norm_k,
                  rwkv_mu, rwkv_w0, rwkv_w_up, rwkv_a0, rwkv_a_up, rwkv_g_up, rwkv_k_k, rwkv_k_a, rwkv_r_k):
    w = w_in[l]
    m0 = CONV_IN
    kr0 = m0 + MLA_Q_LORA + MLA_KV_LORA
    kr = w[:, kr0:kr0 + MLA_ROPE]
    h = MLA_ROPE // 2
    zeros = lambda c: jnp.zeros((D_MODEL, c), F32)
    kr_tile = jnp.concatenate([zeros(MLA_NOPE), kr, zeros(LANES - MLA_QK)], axis=1)
    krs_tile = jnp.concatenate([zeros(MLA_NOPE), kr[:, h:], kr[:, :h], zeros(LANES - MLA_QK)], axis=1)
    w_in_p = jnp.concatenate([w[:, :kr0], kr_tile, krs_tile, w[:, kr0 + MLA_ROPE:]], axis=1).astype(BF16)

    wq = mla_w_uq[l].reshape(MLA_Q_LORA, MLA_HEADS, MLA_QK)
    wkv = mla_w_ukv[l].reshape(MLA_KV_LORA, MLA_HEADS, MLA_NOPE + MLA_V)
    flat = lambda a: _pad_last(a, HEAD_PAD).reshape(a.shape[0], MLA_HEADS * HEAD_PAD).astype(BF16)
    mla = dict(
        gql=mla_q_norm_g[l].reshape(1, -1), gkvl=mla_kv_norm_g[l].reshape(1, -1),
        wq=flat(wq), wqs=flat(_swap_rope(wq)),
        wk=flat(wkv[..., :MLA_NOPE]), wv=flat(wkv[..., MLA_NOPE:]),
        gq=_pad_last(qk_norm_q[l], LANES).reshape(1, LANES),
        gqs=_pad_last(_swap_rope(qk_norm_q[l]), LANES).reshape(1, LANES),
        gk=_pad_last(qk_norm_k[l], LANES).reshape(1, LANES),
        gks=_pad_last(_swap_rope(qk_norm_k[l]), LANES).reshape(1, LANES))

    zl = jnp.zeros((LANES // 2, RWKV_CH), F32)
    hid = jnp.arange(RWKV_CH) // RWKV_HEAD
    rwkv = dict(
        mu=rwkv_mu[l].reshape(1, -1), w0=rwkv_w0[l].reshape(1, -1), a0=rwkv_a0[l].reshape(1, -1),
        wup=jnp.concatenate([rwkv_w_up[l], zl], axis=0).astype(BF16),
        aup=jnp.concatenate([zl, rwkv_a_up[l]], axis=0).astype(BF16),
        gup=rwkv_g_up[l].astype(BF16),
        kk=rwkv_k_k[l].reshape(1, -1), ka=rwkv_k_a[l].reshape(1, -1), rk=rwkv_r_k[l].reshape(1, -1),
        ones=(hid[:, None] == hid[None, :]).astype(BF16))
    return w_in_p, mla, rwkv


def kernel(x, c, positions, ada_w, ada_b, ada_layer_bias, norm1_g, norm2_g, w_in, w_out, conv_w, conv_b, conv_ln_g, conv_ln_b, mla_q_norm_g, mla_w_uq, mla_kv_norm_g, mla_w_ukv, qk_norm_q, qk_norm_k, rwkv_mu, rwkv_w0, rwkv_w_up, rwkv_a0, rwkv_a_up, rwkv_g_up, rwkv_k_k, rwkv_k_a, rwkv_r_k, rwkv_ln_g, rwkv_ln_b, ffn_w_gate, ffn_w_up, ffn_w_down, moe_router, moe_w_gate, moe_w_up, moe_w_down):
    b, t, d = x.shape
    n = b * t
    mods = _cond(c, ada_w, ada_b, ada_layer_bias)
    cos_t, sin_t = _rope_tables(positions)
    x2 = x.reshape(n, d)
    for l in range(DEPTH):
        sh1, sc1, gt1, sh2, sc2, gt2 = [mods[l, :, k * d:(k + 1) * d].reshape(b, 1, d) for k in range(N_MOD)]
        w_in_p, mla_p, rwkv_p = _layer_params(
            l, w_in, mla_q_norm_g, mla_w_uq, mla_kv_norm_g, mla_w_ukv, qk_norm_q, qk_norm_k,
            rwkv_mu, rwkv_w0, rwkv_w_up, rwkv_a0, rwkv_a_up, rwkv_g_up, rwkv_k_k, rwkv_k_a, rwkv_r_k)
        u_conv, u_mla, u_rwkv = _in_proj(x2, sc1, sh1, norm1_g[l].reshape(1, d), w_in_p, t)
        y_conv = _conformer_conv(u_conv.reshape(b, t, CONV_IN), conv_w[l], conv_b[l], conv_ln_g[l], conv_ln_b[l])
        q, kt, v = _mla_prep(u_mla.reshape(b, t, MLA_PAD_IN), cos_t, sin_t, mla_p)
        y_mla = _attention(q, kt, v)
        r_, lw_, k_, v_, a_, b_, bonus, gate = _rwkv_prep(u_rwkv.reshape(b, t, RWKV_IN), rwkv_p)
        qc, y0, g_, f_ = _rwkv_intra(r_, lw_, k_, v_, a_, b_)
        y_rwkv = _rwkv_seq(qc, y0, g_, f_, bonus, gate, rwkv_ln_g[l].reshape(1, -1), rwkv_ln_b[l].reshape(1, -1),
                           rwkv_p["ones"])
        i = l // 2
        router = None
        if l % 2 == 1:
            router = _pad_last(moe_router[i], LANES)
        outs = _out_proj(x2, y_conv.reshape(n, CONV_CH), y_mla.reshape(n, MLA_OUT), y_rwkv.reshape(n, RWKV_CH),
                         w_out[l].astype(BF16), gt1, norm2_g[l].reshape(1, d), sc2, sh2, router, t)
        if l % 2 == 0:
            x2, h2 = outs
            x2 = _dense_ffn(h2, x2, gt2, ffn_w_gate[i].astype(BF16), ffn_w_up[i].astype(BF16),
                            ffn_w_down[i].astype(BF16), t)
        else:
            x2, h2, logits = outs
            x2 = _moe(h2, x2, logits, gt2, moe_w_gate[i].astype(BF16), moe_w_up[i].astype(BF16),
                      moe_w_down[i].astype(BF16), t)
    return x2.reshape(b, t, d)
```

```python
import functools

import jax
import jax.numpy as jnp
from jax import lax
from jax.experimental import pallas as pl
from jax.experimental.pallas import tpu as pltpu

F32 = jnp.float32
BF16 = jnp.bfloat16
I32 = jnp.int32

D_MODEL = 1024
DEPTH = 4
N_MOD = 6
CONV_CH = 256
CONV_K = 31
CONV_IN = 2 * CONV_CH
MLA_NOPE = 64
MLA_ROPE = 32
MLA_V = 64
MLA_QK = MLA_NOPE + MLA_ROPE
MLA_HEADS = 8
MLA_OUT = MLA_HEADS * MLA_V
MLA_Q_LORA = 256
MLA_KV_LORA = 128
MLA_IN = MLA_Q_LORA + MLA_KV_LORA + MLA_ROPE
ROPE_THETA = 10000.0
RWKV_HEAD = 64
RWKV_CH = 256
RWKV_HEADS = 4
RWKV_IN = 1024
N_EXPERTS = 8
RMS_EPS = 1e-6
LN_EPS = 1e-5
RWKV_GN_EPS = 64e-5

LANES = 128
HEAD_PAD = LANES
MLA_PAD_IN = MLA_Q_LORA + MLA_KV_LORA + 2 * LANES
IN_PAD = CONV_IN + MLA_PAD_IN + RWKV_IN
CHUNK = 64
MOE_TILE = 512
_ROW_TILE = D_MODEL // LANES
VMEM_LIMIT = 56 * 1024 * 1024

_NEG = -1e30


def _cp(n_axes, vmem=None):
    return pltpu.CompilerParams(dimension_semantics=("arbitrary",) * n_axes,
                                vmem_limit_bytes=vmem if vmem else VMEM_LIMIT)


def _mm(a, b):
    return jnp.dot(a.astype(BF16), b.astype(BF16), preferred_element_type=F32)


def _mm_nt(a, b):
    return lax.dot_general(a.astype(BF16), b.astype(BF16), (((1,), (1,)), ((), ())),
                           preferred_element_type=F32)


def _mm_tn(a, b):
    return lax.dot_general(a.astype(BF16), b.astype(BF16), (((0,), (0,)), ((), ())),
                           preferred_element_type=F32)


def _sigmoid(x):
    return 1.0 / (1.0 + jnp.exp(-x))


def _split2(x):
    hi = x.astype(BF16)
    lo = (x - hi.astype(F32)).astype(BF16)
    return hi, lo


def _seg_sum(x, ones_bd):
    hi, lo = _split2(x)
    return (jnp.dot(hi, ones_bd, preferred_element_type=F32)
            + jnp.dot(lo, ones_bd, preferred_element_type=F32))


def _cond_kernel(c_ref, w_ref, b_ref, lb_ref, o_ref):
    c = c_ref[...]
    s = c * _sigmoid(c)
    base = jnp.dot(s, w_ref[...], precision=lax.Precision.HIGHEST, preferred_element_type=F32) + b_ref[...]
    for l in range(DEPTH):
        o_ref[l] = base + lb_ref[l]


def _cond(c, ada_w, ada_b, ada_layer_bias):
    b = c.shape[0]
    n = ada_w.shape[1]
    tn = 1536
    cp = jnp.zeros((8, D_MODEL), F32).at[:b].set(c)
    out = pl.pallas_call(
        _cond_kernel,
        grid=(n // tn,),
        in_specs=[pl.BlockSpec((8, D_MODEL), lambda j: (0, 0)),
                  pl.BlockSpec((D_MODEL, tn), lambda j: (0, j)),
                  pl.BlockSpec((1, tn), lambda j: (0, j)),
                  pl.BlockSpec((DEPTH, 1, tn), lambda j: (0, 0, j))],
        out_specs=pl.BlockSpec((DEPTH, 8, tn), lambda j: (0, 0, j)),
        out_shape=jax.ShapeDtypeStruct((DEPTH, 8, n), F32),
        compiler_params=_cp(1),
        name="cond",
    )(cp, ada_w, ada_b.reshape(1, n), ada_layer_bias.reshape(DEPTH, 1, n))
    return out[:, :b]


def _rope_kernel(pos_ref, invf_ref, c_ref, s_ref):
    pos = pos_ref[0].astype(F32)
    ang = pos * invf_ref[...]
    lane = lax.broadcasted_iota(I32, ang.shape, 1)
    cs = jnp.cos(ang)
    sn = jnp.sin(ang)
    c_ref[0] = jnp.where(lane < MLA_NOPE, 1.0, jnp.where(lane < MLA_QK, cs, 0.0))
    s_ref[0] = jnp.where((lane >= MLA_NOPE) & (lane < MLA_NOPE + 16), -sn,
                         jnp.where((lane >= MLA_NOPE + 16) & (lane < MLA_QK), sn, 0.0))


def _rope_tables(positions):
    b, t = positions.shape
    tt = min(512, t)
    inv_freq = ROPE_THETA ** (-jnp.arange(0, MLA_ROPE, 2, dtype=F32) / MLA_ROPE)
    invf = jnp.zeros((1, LANES), F32)
    invf = invf.at[0, MLA_NOPE:MLA_NOPE + 16].set(inv_freq).at[0, MLA_NOPE + 16:MLA_QK].set(inv_freq)
    spec = pl.BlockSpec((1, tt, LANES), lambda bi, i: (bi, i, 0))
    return pl.pallas_call(
        _rope_kernel,
        grid=(b, t // tt),
        in_specs=[pl.BlockSpec((1, tt, 1), lambda bi, i: (bi, i, 0)),
                  pl.BlockSpec((1, LANES), lambda bi, i: (0, 0))],
        out_specs=[spec, spec],
        out_shape=[jax.ShapeDtypeStruct((b, t, LANES), F32)] * 2,
        compiler_params=_cp(2),
        name="rope_tables",
    )(positions.reshape(b, t, 1), invf)


def _in_proj_kernel(x_ref, sc_ref, sh_ref, g_ref, w_ref, oc_ref, om_ref, or_ref):
    x = x_ref[...]
    ms = jnp.mean(x * x, axis=-1, keepdims=True)
    h = x * lax.rsqrt(ms + RMS_EPS) * g_ref[...]
    h = h * (1.0 + sc_ref[0]) + sh_ref[0]
    hb = h.astype(BF16)
    oc_ref[...] = jnp.dot(hb, w_ref[:, 0:CONV_IN], preferred_element_type=F32)
    om_ref[...] = jnp.dot(hb, w_ref[:, CONV_IN:CONV_IN + MLA_PAD_IN], preferred_element_type=F32)
    or_ref[...] = jnp.dot(hb, w_ref[:, CONV_IN + MLA_PAD_IN:IN_PAD], preferred_element_type=F32)


def _in_proj(x2, sc, sh, g, w, t):
    n = x2.shape[0]
    tm = min(512, t)
    tpb = t // tm
    vec = pl.BlockSpec((1, 1, D_MODEL), lambda i: (i // tpb, 0, 0))
    return pl.pallas_call(
        _in_proj_kernel,
        grid=(n // tm,),
        in_specs=[pl.BlockSpec((tm, D_MODEL), lambda i: (i, 0)), vec, vec,
                  pl.BlockSpec((1, D_MODEL), lambda i: (0, 0)),
                  pl.BlockSpec((D_MODEL, IN_PAD), lambda i: (0, 0))],
        out_specs=[pl.BlockSpec((tm, CONV_IN), lambda i: (i, 0)),
                   pl.BlockSpec((tm, MLA_PAD_IN), lambda i: (i, 0)),
                   pl.BlockSpec((tm, RWKV_IN), lambda i: (i, 0))],
        out_shape=[jax.ShapeDtypeStruct((n, CONV_IN), F32),
                   jax.ShapeDtypeStruct((n, MLA_PAD_IN), F32),
                   jax.ShapeDtypeStruct((n, RWKV_IN), F32)],
        compiler_params=_cp(1),
        name="in_proj",
    )(x2, sc, sh, g, w)


_CONV_HALO = 32


def _conv_kernel(u_ref, up_ref, w_ref, b_ref, g_ref, bb_ref, o_ref, zbuf, *, tt):
    i = pl.program_id(1)
    u = u_ref[0]
    z = u[:, :CONV_CH] * _sigmoid(u[:, CONV_CH:])
    up = up_ref[0]
    zp = up[:, :CONV_CH] * _sigmoid(up[:, CONV_CH:])
    zp = jnp.where(i > 0, zp, 0.0)
    zbuf[0:_CONV_HALO, :] = zp
    zbuf[_CONV_HALO:, :] = z
    acc = jnp.zeros((tt, CONV_CH), F32)
    for k in range(CONV_K):
        acc = acc + zbuf[pl.ds(_CONV_HALO - (CONV_K - 1) + k, tt), :] * w_ref[k:k + 1, :]
    y = acc + b_ref[...]
    mu = jnp.mean(y, axis=-1, keepdims=True)
    yc = y - mu
    var = jnp.mean(yc * yc, axis=-1, keepdims=True)
    yn = yc * lax.rsqrt(var + LN_EPS) * g_ref[...] + bb_ref[...]
    o_ref[0] = yn * _sigmoid(yn)


def _conformer_conv(u3, conv_w, conv_b, ln_g, ln_b):
    b, t, _ = u3.shape
    tt = min(512, t)
    hb = tt // _CONV_HALO
    vec = pl.BlockSpec((1, CONV_CH), lambda bi, i: (0, 0))
    return pl.pallas_call(
        functools.partial(_conv_kernel, tt=tt),
        grid=(b, t // tt),
        in_specs=[pl.BlockSpec((1, tt, CONV_IN), lambda bi, i: (bi, i, 0)),
                  pl.BlockSpec((1, _CONV_HALO, CONV_IN), lambda bi, i: (bi, jnp.maximum(i * hb - 1, 0), 0)),
                  pl.BlockSpec((CONV_K, CONV_CH), lambda bi, i: (0, 0)), vec, vec, vec],
        out_specs=pl.BlockSpec((1, tt, CONV_CH), lambda bi, i: (bi, i, 0)),
        out_shape=jax.ShapeDtypeStruct((b, t, CONV_CH), F32),
        scratch_shapes=[pltpu.VMEM((tt + _CONV_HALO, CONV_CH), F32)],
        compiler_params=_cp(2),
        name="conformer_conv",
    )(u3, u3, conv_w, conv_b.reshape(1, -1), ln_g.reshape(1, -1), ln_b.reshape(1, -1))


def _mla_prep_kernel(um_ref, c_ref, s_ref, gql_ref, wq_ref, wqs_ref, gkvl_ref, wk_ref, wv_ref,
                     gq_ref, gqs_ref, gk_ref, gks_ref, q_ref, kt_ref, v_ref):
    um = um_ref[0]
    cq = um[:, 0:MLA_Q_LORA]
    ckv = um[:, MLA_Q_LORA:MLA_Q_LORA + MLA_KV_LORA]
    kr = um[:, MLA_Q_LORA + MLA_KV_LORA:MLA_Q_LORA + MLA_KV_LORA + LANES]
    krs = um[:, MLA_Q_LORA + MLA_KV_LORA + LANES:MLA_PAD_IN]
    cqn = (cq * lax.rsqrt(jnp.mean(cq * cq, axis=-1, keepdims=True) + RMS_EPS) * gql_ref[...]).astype(BF16)
    ckvn = (ckv * lax.rsqrt(jnp.mean(ckv * ckv, axis=-1, keepdims=True) + RMS_EPS) * gkvl_ref[...]).astype(BF16)
    q = jnp.dot(cqn, wq_ref[...], preferred_element_type=F32)
    qs = jnp.dot(cqn, wqs_ref[...], preferred_element_type=F32)
    kn = jnp.dot(ckvn, wk_ref[...], preferred_element_type=F32)
    vv = jnp.dot(ckvn, wv_ref[...], preferred_element_type=F32)
    cos_t = c_ref[0]
    sin_t = s_ref[0]
    lane = lax.broadcasted_iota(I32, cos_t.shape, 1)
    scale = float(MLA_QK) ** -0.5
    inv_d = 1.0 / MLA_QK
    for h in range(MLA_HEADS):
        sl = slice(h * HEAD_PAD, (h + 1) * HEAD_PAD)
        qh = q[:, sl]
        r = lax.rsqrt(jnp.sum(qh * qh, axis=-1, keepdims=True) * inv_d + RMS_EPS)
        qf = qh * (r * gq_ref[...]) * cos_t + qs[:, sl] * (r * gqs_ref[...]) * sin_t
        q_ref[0, :, sl] = (qf * scale).astype(BF16)
        kh = kn[:, sl] + kr
        r = lax.rsqrt(jnp.sum(kh * kh, axis=-1, keepdims=True) * inv_d + RMS_EPS)
        kf = kh * (r * gk_ref[...]) * cos_t + krs * (r * gks_ref[...]) * sin_t
        kt_ref[0, h] = kf.T.astype(BF16)
        v_ref[0, h] = jnp.where(lane == MLA_V, 1.0, vv[:, sl]).astype(BF16)


def _mla_prep(um3, cos_t, sin_t, p):
    b, t, _ = um3.shape
    tt = min(256, t)
    full = lambda shape: pl.BlockSpec(shape, lambda bi, i: (0,) * len(shape))
    tab = pl.BlockSpec((1, tt, LANES), lambda bi, i: (bi, i, 0))
    return pl.pallas_call(
        _mla_prep_kernel,
        grid=(b, t // tt),
        in_specs=[pl.BlockSpec((1, tt, MLA_PAD_IN), lambda bi, i: (bi, i, 0)), tab, tab,
                  full((1, MLA_Q_LORA)), full((MLA_Q_LORA, MLA_HEADS * HEAD_PAD)),
                  full((MLA_Q_LORA, MLA_HEADS * HEAD_PAD)),
                  full((1, MLA_KV_LORA)), full((MLA_KV_LORA, MLA_HEADS * HEAD_PAD)),
                  full((MLA_KV_LORA, MLA_HEADS * HEAD_PAD)),
                  full((1, LANES)), full((1, LANES)), full((1, LANES)), full((1, LANES))],
        out_specs=[pl.BlockSpec((1, tt, MLA_HEADS * HEAD_PAD), lambda bi, i: (bi, i, 0)),
                   pl.BlockSpec((1, MLA_HEADS, HEAD_PAD, tt), lambda bi, i: (bi, 0, 0, i)),
                   pl.BlockSpec((1, MLA_HEADS, tt, HEAD_PAD), lambda bi, i: (bi, 0, i, 0))],
        out_shape=[jax.ShapeDtypeStruct((b, t, MLA_HEADS * HEAD_PAD), BF16),
                   jax.ShapeDtypeStruct((b, MLA_HEADS, HEAD_PAD, t), BF16),
                   jax.ShapeDtypeStruct((b, MLA_HEADS, t, HEAD_PAD), BF16)],
        compiler_params=_cp(2),
        name="mla_prep",
    )(um3, cos_t, sin_t, p["gql"], p["wq"], p["wqs"], p["gkvl"], p["wk"], p["wv"],
      p["gq"], p["gqs"], p["gk"], p["gks"])


_ATTN_HEAD_GROUP = 4
def _attn_kernel(q_ref, kt_ref, v_ref, o_ref, *, tq, tk):
    i = pl.program_id(1)
    nfull = i * (tq // tk)
    row = lax.broadcasted_iota(I32, (tq, tk), 0)
    col = lax.broadcasted_iota(I32, (tq, tk), 1)
    outs = []

    def step(h, off, m, acc, mask):
        q = q_ref[0, :, h * HEAD_PAD:(h + 1) * HEAD_PAD]
        s = jnp.dot(q, kt_ref[0, h, :, pl.ds(off, tk)], preferred_element_type=F32)
        if mask is not None:
            s = jnp.where(mask, s, _NEG)
        m_new = jnp.maximum(m, jnp.max(s, axis=-1, keepdims=True))
        p = jnp.exp(s - m_new)
        acc = acc * jnp.exp(m - m_new) + jnp.dot(p.astype(BF16), v_ref[0, h, pl.ds(off, tk), :],
                                                 preferred_element_type=F32)
        return m_new, acc

    for h0 in range(0, MLA_HEADS, _ATTN_HEAD_GROUP):
        heads = range(h0, h0 + _ATTN_HEAD_GROUP)

        def body(kb, carry):
            off = pl.multiple_of(kb * tk, tk)
            return tuple(step(h, off, carry[n][0], carry[n][1], None) for n, h in enumerate(heads))

        init = tuple((jnp.full((tq, 1), _NEG, F32), jnp.zeros((tq, HEAD_PAD), F32)) for _ in heads)
        carry = lax.fori_loop(0, nfull, body, init)
        for d in range(tq // tk):
            off = pl.multiple_of(i * tq + d * tk, tk)
            carry = tuple(step(h, off, carry[n][0], carry[n][1], col + d * tk <= row)
                          for n, h in enumerate(heads))
        for m, acc in carry:
            outs.append(acc[:, :MLA_V] / acc[:, MLA_V:MLA_V + 1])
    o_ref[0] = jnp.concatenate(outs, axis=-1)


def _attention(q, kt, v):
    b, t, _ = q.shape
    tq = min(512, t)
    tk = min(256, t)
    return pl.pallas_call(
        functools.partial(_attn_kernel, tq=tq, tk=tk),
        grid=(b, t // tq),
        in_specs=[pl.BlockSpec((1, tq, MLA_HEADS * HEAD_PAD), lambda bi, i: (bi, i, 0)),
                  pl.BlockSpec((1, MLA_HEADS, HEAD_PAD, t), lambda bi, i: (bi, 0, 0, 0)),
                  pl.BlockSpec((1, MLA_HEADS, t, HEAD_PAD), lambda bi, i: (bi, 0, 0, 0))],
        out_specs=pl.BlockSpec((1, tq, MLA_OUT), lambda bi, i: (bi, i, 0)),
        out_shape=jax.ShapeDtypeStruct((b, t, MLA_OUT), F32),
        compiler_params=_cp(2),
        name="mla_attention",
    )(q, kt, v)


def _rwkv_prep_kernel(u_ref, up_ref, mu_ref, w0_ref, wup_ref, a0_ref, aup_ref, gup_ref, kk_ref, ka_ref,
                      rk_ref, ones_ref, r_o, lw_o, k_o, v_o, a_o, b_o, bonus_o, g_o):
    i = pl.program_id(1)
    u = u_ref[0]
    prow = jnp.where(i > 0, up_ref[0][7:8, :], 0.0)
    rowi = lax.broadcasted_iota(I32, u.shape, 0)
    prev = jnp.where(rowi == 0, prow, pltpu.roll(u, 1, axis=0))
    uf = u + (prev - u) * mu_ref[...]
    c = RWKV_CH
    r = uf[:, 0:c]
    k = uf[:, c:2 * c]
    v = uf[:, 2 * c:3 * c]
    wa = uf[:, 3 * c:3 * c + LANES]
    gd = uf[:, 3 * c + LANES:]
    ones_bd = ones_ref[...]
    wl = w0_ref[...] + _mm(jnp.tanh(wa), wup_ref[...])
    z = -wl
    softplus = jnp.maximum(z, 0.0) + jnp.log(1.0 + jnp.exp(-jnp.abs(z)))
    lw = -jnp.exp(-softplus - 0.5)
    a_s = _sigmoid(a0_ref[...] + _mm(wa, aup_ref[...]))
    g = _mm(_sigmoid(gd), gup_ref[...])
    kk = k * kk_ref[...]
    nrm = jnp.maximum(jnp.sqrt(_seg_sum(kk * kk, ones_bd)), 1e-12)
    kkn = kk / nrm
    k2 = k * (1.0 + (a_s - 1.0) * ka_ref[...])
    r_o[0] = r
    lw_o[0] = lw
    k_o[0] = k2
    v_o[0] = v
    a_o[0] = -kkn
    b_o[0] = kkn * a_s
    bonus_o[0] = _seg_sum(r * k2 * rk_ref[...], ones_bd) * v
    g_o[0] = g


def _rwkv_prep(u3, p):
    b, t, _ = u3.shape
    tt = min(512, t)
    hb = tt // 8
    full = lambda shape: pl.BlockSpec(shape, lambda bi, i: (0,) * len(shape))
    out = pl.BlockSpec((1, tt, RWKV_CH), lambda bi, i: (bi, i, 0))
    return pl.pallas_call(
        _rwkv_prep_kernel,
        grid=(b, t // tt),
        in_specs=[pl.BlockSpec((1, tt, RWKV_IN), lambda bi, i: (bi, i, 0)),
                  pl.BlockSpec((1, 8, RWKV_IN), lambda bi, i: (bi, jnp.maximum(i * hb - 1, 0), 0)),
                  full((1, RWKV_IN)), full((1, RWKV_CH)), full((LANES, RWKV_CH)), full((1, RWKV_CH)),
                  full((LANES, RWKV_CH)), full((LANES, RWKV_CH)), full((1, RWKV_CH)), full((1, RWKV_CH)),
                  full((1, RWKV_CH)), full((RWKV_CH, RWKV_CH))],
        out_specs=[out] * 8,
        out_shape=[jax.ShapeDtypeStruct((b, t, RWKV_CH), F32)] * 8,
        compiler_params=_cp(2),
        name="rwkv_prep",
    )(u3, u3, p["mu"], p["w0"], p["wup"], p["a0"], p["aup"], p["gup"], p["kk"], p["ka"], p["rk"], p["ones"])


def _intra_chunks(rs, lws, ks, vs, as_, bs, mask_bd, eye_bd, tril, strict, incl, eye_p):
    c = CHUNK
    each = lambda fn, *lists: [fn(*xs) for xs in zip(*lists)]

    def bd(x):
        xb = x.astype(BF16)
        return jnp.where(mask_bd, jnp.concatenate([xb, xb, xb, xb], axis=0), jnp.zeros((), BF16))

    def prefix(lw):
        hi = lw.astype(BF16)
        rem = lw - hi.astype(F32)
        mid = rem.astype(BF16)
        lo = (rem - mid.astype(F32)).astype(BF16)
        return (jnp.dot(tril, hi, preferred_element_type=F32) + jnp.dot(tril, mid, preferred_element_type=F32)
                + jnp.dot(tril, lo, preferred_element_type=F32))

    cums = each(prefix, lws)
    tots = each(lambda cum: cum[c - 1:c, :], cums)
    e_negs = each(lambda cum: jnp.exp(-cum), cums)
    e_ends = each(lambda tot, cum: jnp.exp(tot - cum), tots, cums)
    rts = each(lambda r, cum: r * jnp.exp(cum), rs, cums)
    ats = each(lambda a, cum, lw: a * jnp.exp(cum - lw), as_, cums, lws)
    stacks = each(lambda at, rt: jnp.concatenate([at, rt], axis=0), ats, rts)
    s1s = each(lambda st, b, e: _mm_nt(st, bd(b * e)), stacks, bs, e_negs)
    s2s = each(lambda st, k, e: _mm_nt(st, bd(k * e)), stacks, ks, e_negs)
    a_abs = each(lambda s1: jnp.where(strict, s1[:c], 0.0), s1s)
    m_rbs = each(lambda s1: jnp.where(incl, s1[c:], 0.0), s1s)
    a_aks = each(lambda s2: jnp.where(strict, s2[:c], 0.0), s2s)
    m_rks = each(lambda s2: jnp.where(incl, s2[c:], 0.0), s2s)
    xs = each(lambda a_ab: eye_p + a_ab, a_abs)
    aks = each(lambda a_ab: _mm(a_ab, bd(a_ab)), a_abs)
    for j in range(5):
        if j < 4:
            pps = each(lambda ak, x: _mm(jnp.concatenate([ak, x], axis=0), bd(ak)), aks, xs)
            xs = each(lambda x, pp: x + pp[c:], xs, pps)
            aks = each(lambda pp: pp[:c], pps)
        else:
            xs = each(lambda x, ak: x + _mm(x, bd(ak)), xs, aks)
    bdvs = each(bd, vs)
    avs = each(_mm, a_aks, bdvs)
    ahs = each(lambda x, at: _mm(x, bd(at)), xs, ats)
    uhs = each(lambda x, av: _mm(x, bd(av)), xs, avs)
    qs = each(lambda rt, m_rb, ah: rt + _mm(m_rb, bd(ah)), rts, m_rbs, ahs)
    y0s = each(lambda m_rb, uh, m_rk, bdv: _mm(m_rb, bd(uh)) + _mm(m_rk, bdv), m_rbs, uhs, m_rks, bdvs)
    gs = each(lambda b, e, ah, tot: jnp.where(mask_bd, _mm_tn(b * e, ah), 0.0) + jnp.where(eye_bd, jnp.exp(tot), 0.0),
              bs, e_ends, ahs, tots)
    fs = each(lambda b, k, e, uh, v: jnp.where(mask_bd, _mm_tn(b * e, uh) + _mm_tn(k * e, v), 0.0),
              bs, ks, e_ends, uhs, vs)
    return qs, y0s, gs, fs


def _rwkv_intra_kernel(r_ref, lw_ref, k_ref, v_ref, a_ref, b_ref, q_o, y0_o, g_o, f_o, *, cps):
    c = CHUNK
    n = RWKV_CH
    ri = lax.broadcasted_iota(I32, (n, n), 0)
    ci = lax.broadcasted_iota(I32, (n, n), 1)
    mask_bd = (ri // RWKV_HEAD) == (ci // RWKV_HEAD)
    eye_bd = ri == ci
    t64 = lax.broadcasted_iota(I32, (c, c), 0)
    i64 = lax.broadcasted_iota(I32, (c, c), 1)
    tril = jnp.where(i64 <= t64, 1.0, 0.0).astype(BF16)
    tp = lax.broadcasted_iota(I32, (c, n), 0)
    ip = lax.broadcasted_iota(I32, (c, n), 1) & (RWKV_HEAD - 1)
    strict = ip < tp
    incl = ip <= tp
    eye_p = jnp.where(ip == tp, 1.0, 0.0)
    sls = [slice(s * c, (s + 1) * c) for s in range(cps)]
    load = lambda ref: [ref[0, sl, :] for sl in sls]
    qs, y0s, gs, fs = _intra_chunks(load(r_ref), load(lw_ref), load(k_ref), load(v_ref), load(a_ref), load(b_ref),
                                    mask_bd, eye_bd, tril, strict, incl, eye_p)
    for s, sl in enumerate(sls):
        q_o[0, sl, :] = qs[s]
        y0_o[0, sl, :] = y0s[s]
        g_o[0, s] = gs[s]
        f_o[0, s] = fs[s]


def _rwkv_intra(r, lw, k, v, a, b_):
    b, t, n = r.shape
    nc = t // CHUNK
    cps = max(d for d in (1, 2, 4, 8) if nc % d == 0)
    tok = pl.BlockSpec((1, cps * CHUNK, n), lambda bi, i: (bi, i, 0))
    mat = pl.BlockSpec((1, cps, n, n), lambda bi, i: (bi, i, 0, 0))
    return pl.pallas_call(
        functools.partial(_rwkv_intra_kernel, cps=cps),
        grid=(b, nc // cps),
        in_specs=[tok] * 6,
        out_specs=[tok, tok, mat, mat],
        out_shape=[jax.ShapeDtypeStruct((b, t, n), F32), jax.ShapeDtypeStruct((b, t, n), F32),
                   jax.ShapeDtypeStruct((b, nc, n, n), F32), jax.ShapeDtypeStruct((b, nc, n, n), F32)],
        compiler_params=_cp(2),
        name="rwkv_intra",
    )(r, lw, k, v, a, b_)


def _rwkv_seq_kernel(q_ref, y0_ref, g_ref, f_ref, o_ref, h_ref, *, nb):
    c = pl.program_id(0)

    @pl.when(c == 0)
    def _():
        h_ref[...] = jnp.zeros_like(h_ref)

    for b in range(nb):
        hb = h_ref[b].astype(BF16)
        o_ref[b] = jnp.dot(q_ref[b].astype(BF16), hb, preferred_element_type=F32) + y0_ref[b]
        h_ref[b] = jnp.dot(g_ref[b, 0].astype(BF16), hb, preferred_element_type=F32) + f_ref[b, 0]


def _rwkv_seq(q, y0, g, f):
    b, t, n = q.shape
    nc = t // CHUNK
    tok = pl.BlockSpec((b, CHUNK, n), lambda c: (0, c, 0))
    mat = pl.BlockSpec((b, 1, n, n), lambda c: (0, c, 0, 0))
    return pl.pallas_call(
        functools.partial(_rwkv_seq_kernel, nb=b),
        grid=(nc,),
        in_specs=[tok, tok, mat, mat],
        out_specs=tok,
        out_shape=jax.ShapeDtypeStruct((b, t, n), F32),
        scratch_shapes=[pltpu.VMEM((b, n, n), F32)],
        compiler_params=_cp(1),
        name="rwkv_seq",
    )(q, y0, g, f)


def _store_token_tiles(ref, val, tm):
    for p in range(D_MODEL // LANES):
        ref[pl.ds(p, tm, stride=_ROW_TILE), :] = val[:, p * LANES:(p + 1) * LANES]


def _load_token_tiles(ref, tm):
    return jnp.concatenate([ref[pl.ds(p, tm, stride=_ROW_TILE), :] for p in range(D_MODEL // LANES)], axis=-1)


def _out_proj_kernel(x_ref, yc_ref, ym_ref, yw_ref, bonus_ref, gate_ref, lng_ref, lnb_ref, ones_ref,
                     w_ref, gt_ref, g2_ref, sc_ref, sh_ref, *rest, with_router, tm):
    if with_router:
        wr_ref, xo_ref, h_ref, lg_ref = rest
    else:
        xo_ref, h_ref = rest
    ones_bd = ones_ref[...]
    inv = 1.0 / RWKV_HEAD
    y = yw_ref[...]
    mu = _seg_sum(y, ones_bd) * inv
    yc = y - mu
    var = _seg_sum(yc * yc, ones_bd) * inv
    yn = yc * lax.rsqrt(var + RWKV_GN_EPS) * lng_ref[...] + lnb_ref[...]
    yr = (yn + bonus_ref[...]) * gate_ref[...]
    mixed = (_mm(yc_ref[...], w_ref[0:CONV_CH, :])
             + _mm(ym_ref[...], w_ref[CONV_CH:CONV_CH + MLA_OUT, :])
             + _mm(yr, w_ref[CONV_CH + MLA_OUT:, :]))
    x = x_ref[...] + gt_ref[0] * mixed
    xo_ref[...] = x
    ms = jnp.mean(x * x, axis=-1, keepdims=True)
    h = x * lax.rsqrt(ms + RMS_EPS) * g2_ref[...]
    h = h * (1.0 + sc_ref[0]) + sh_ref[0]
    if with_router:
        _store_token_tiles(h_ref, h, tm)
    else:
        h_ref[...] = h
    if with_router:
        h_hi, h_lo = _split2(h)
        w_hi, w_lo = _split2(wr_ref[...])
        lg_ref[...] = (jnp.dot(h_hi, w_hi, preferred_element_type=F32)
                       + jnp.dot(h_lo, w_hi, preferred_element_type=F32)
                       + jnp.dot(h_hi, w_lo, preferred_element_type=F32))


def _out_proj(x2, yc, ym, yw, bonus, gate, ln_g, ln_b, ones_bd, w_out, gt, g2, sc, sh, router, t):
    n = x2.shape[0]
    tm = min(512, t)
    tpb = t // tm
    vec = pl.BlockSpec((1, 1, D_MODEL), lambda i: (i // tpb, 0, 0))
    row = lambda w: pl.BlockSpec((tm, w), lambda i: (i, 0))
    full = lambda r, c: pl.BlockSpec((r, c), lambda i: (0, 0))
    in_specs = [row(D_MODEL), row(CONV_CH), row(MLA_OUT), row(RWKV_CH), row(RWKV_CH), row(RWKV_CH),
                full(1, RWKV_CH), full(1, RWKV_CH), full(RWKV_CH, RWKV_CH),
                full(D_MODEL, D_MODEL), vec, full(1, D_MODEL), vec, vec]
    out_specs = [row(D_MODEL), row(D_MODEL)]
    out_shape = [jax.ShapeDtypeStruct((n, D_MODEL), F32), jax.ShapeDtypeStruct((n, D_MODEL), F32)]
    args = [x2, yc, ym, yw, bonus, gate, ln_g, ln_b, ones_bd, w_out, gt, g2, sc, sh]
    if router is not None:
        in_specs.append(full(D_MODEL, LANES))
        out_specs = [row(D_MODEL), pl.BlockSpec((tm * _ROW_TILE, LANES), lambda i: (i, 0)), row(LANES)]
        out_shape = [out_shape[0], jax.ShapeDtypeStruct((n * _ROW_TILE, LANES), F32),
                     jax.ShapeDtypeStruct((n, LANES), F32)]
        args.append(router)
    return pl.pallas_call(
        functools.partial(_out_proj_kernel, with_router=router is not None, tm=tm),
        grid=(n // tm,),
        in_specs=in_specs, out_specs=out_specs, out_shape=out_shape,
        compiler_params=_cp(1),
        name="out_proj",
    )(*args)


def _swiglu_step(load_h, wg_ref, wu_ref, wd_ref, hb_ref, acc_ref, j):
    @pl.when(j == 0)
    def _():
        hb_ref[...] = load_h().astype(BF16)
        acc_ref[...] = jnp.zeros_like(acc_ref)

    hb = hb_ref[...]
    gate = jnp.dot(hb, wg_ref[...], preferred_element_type=F32)
    up = jnp.dot(hb, wu_ref[...], preferred_element_type=F32)
    act = (gate * _sigmoid(gate) * up).astype(BF16)
    acc_ref[...] += jnp.dot(act, wd_ref[...], preferred_element_type=F32)


def _ffn_kernel(h_ref, x_ref, gt_ref, wg_ref, wu_ref, wd_ref, o_ref, hb_ref, acc_ref, *, nj):
    j = pl.program_id(1)
    _swiglu_step(lambda: h_ref[...], wg_ref, wu_ref, wd_ref, hb_ref, acc_ref, j)

    @pl.when(j == nj - 1)
    def _():
        o_ref[...] = x_ref[...] + gt_ref[0] * acc_ref[...]


def _dense_ffn(h2, x2, gt, wg, wu, wd, t):
    n = x2.shape[0]
    ff = wg.shape[1]
    tm = min(512, t)
    tpb = t // tm
    nj = 2
    tf = ff // nj
    row = pl.BlockSpec((tm, D_MODEL), lambda i, j: (i, 0))
    return pl.pallas_call(
        functools.partial(_ffn_kernel, nj=nj),
        grid=(n // tm, nj),
        in_specs=[row, row, pl.BlockSpec((1, 1, D_MODEL), lambda i, j: (i // tpb, 0, 0)),
                  pl.BlockSpec((D_MODEL, tf), lambda i, j: (0, j)),
                  pl.BlockSpec((D_MODEL, tf), lambda i, j: (0, j)),
                  pl.BlockSpec((tf, D_MODEL), lambda i, j: (j, 0))],
        out_specs=row,
        out_shape=jax.ShapeDtypeStruct((n, D_MODEL), F32),
        scratch_shapes=[pltpu.VMEM((tm, D_MODEL), BF16), pltpu.VMEM((tm, D_MODEL), F32)],
        compiler_params=_cp(2),
        name="dense_ffn",
    )(h2, x2, gt, wg, wu, wd)


def _moe_ffn_kernel(te_ref, nu_ref, xs_ref, wg_ref, wu_ref, wd_ref, o_ref, hb_ref, acc_ref, *, nj, tm):
    i = pl.program_id(0)
    j = pl.program_id(1)
    used = i < nu_ref[0]

    @pl.when(used)
    def _():
        _swiglu_step(lambda: _load_token_tiles(xs_ref, tm), wg_ref.at[0], wu_ref.at[0], wd_ref.at[0],
                     hb_ref, acc_ref, j)

    @pl.when(j == nj - 1)
    def _():
        _store_token_tiles(o_ref, jnp.where(used, acc_ref[...], 0.0), tm)


def _moe_ffn(xs, tile_e, n_used, wg, wu, wd):
    cap = xs.shape[0] // _ROW_TILE
    ff = wg.shape[2]
    nj = 2
    tf = ff // nj
    tm = MOE_TILE

    def wj(i, j, nu):
        return jnp.where(i < nu[0], j, nj - 1)

    row = pl.BlockSpec((tm * _ROW_TILE, LANES), lambda i, j, te, nu: (i, 0))
    grid_spec = pltpu.PrefetchScalarGridSpec(
        num_scalar_prefetch=2,
        grid=(cap // tm, nj),
        in_specs=[row,
                  pl.BlockSpec((1, D_MODEL, tf), lambda i, j, te, nu: (te[i], 0, wj(i, j, nu))),
                  pl.BlockSpec((1, D_MODEL, tf), lambda i, j, te, nu: (te[i], 0, wj(i, j, nu))),
                  pl.BlockSpec((1, tf, D_MODEL), lambda i, j, te, nu: (te[i], wj(i, j, nu), 0))],
        out_specs=row,
        scratch_shapes=[pltpu.VMEM((tm, D_MODEL), BF16), pltpu.VMEM((tm, D_MODEL), F32)])
    return pl.pallas_call(
        functools.partial(_moe_ffn_kernel, nj=nj, tm=tm),
        grid_spec=grid_spec,
        out_shape=jax.ShapeDtypeStruct((cap * _ROW_TILE, LANES), F32),
        compiler_params=_cp(2),
        name="moe_ffn",
    )(tile_e, n_used, xs, wg, wu, wd)


def _route_kernel(lg_ref, oi_ref, of_ref, cnt_ref, carry_ref, *, tm):
    i = pl.program_id(0)

    @pl.when(i == 0)
    def _():
        carry_ref[...] = jnp.zeros_like(carry_ref)

    lane = lax.broadcasted_iota(I32, (tm, LANES), 1)
    real = lane < N_EXPERTS
    lg = jnp.where(real, lg_ref[...], _NEG)
    ex = jnp.exp(lg - jnp.max(lg, axis=-1, keepdims=True))
    p = ex / jnp.sum(ex, axis=-1, keepdims=True)
    p = jnp.where(real, p, -1.0)
    m1 = jnp.max(p, axis=-1, keepdims=True)
    i1 = jnp.min(jnp.where(p == m1, lane, LANES), axis=-1, keepdims=True)
    p2 = jnp.where(lane == i1, -1.0, p)
    m2 = jnp.max(p2, axis=-1, keepdims=True)
    i2 = jnp.min(jnp.where(p2 == m2, lane, LANES), axis=-1, keepdims=True)
    den = m1 + m2
    oh1 = lane == i1
    oh2 = lane == i2
    oh = jnp.where(oh1 | oh2, 1.0, 0.0)
    rr = lax.broadcasted_iota(I32, (tm, tm), 0)
    cc = lax.broadcasted_iota(I32, (tm, tm), 1)
    lstrict = jnp.where(cc < rr, 1.0, 0.0).astype(BF16)
    pre = jnp.dot(lstrict, oh.astype(BF16), preferred_element_type=F32) + carry_ref[...]
    rank1 = jnp.sum(jnp.where(oh1, pre, 0.0), axis=-1, keepdims=True).astype(I32)
    rank2 = jnp.sum(jnp.where(oh2, pre, 0.0), axis=-1, keepdims=True).astype(I32)
    carry_ref[...] = carry_ref[...] + jnp.sum(oh, axis=0, keepdims=True)
    cnt_ref[...] = carry_ref[...]
    oi_ref[...] = jnp.where(lane == 0, i1, jnp.where(lane == 1, i2, jnp.where(lane == 2, rank1,
                                                                                jnp.where(lane == 3, rank2, 0))))
    of_ref[...] = jnp.where(lane == 0, m1 / den, jnp.where(lane == 1, m2 / den, 0.0))


def _route(logits, t):
    n = logits.shape[0]
    tm = min(512, t)
    row = pl.BlockSpec((tm, LANES), lambda i: (i, 0))
    return pl.pallas_call(
        functools.partial(_route_kernel, tm=tm),
        grid=(n // tm,),
        in_specs=[row],
        out_specs=[row, row, pl.BlockSpec((1, LANES), lambda i: (0, 0))],
        out_shape=[jax.ShapeDtypeStruct((n, LANES), I32), jax.ShapeDtypeStruct((n, LANES), F32),
                   jax.ShapeDtypeStruct((1, LANES), F32)],
        scratch_shapes=[pltpu.VMEM((1, LANES), F32)],
        compiler_params=_cp(1),
        name="moe_route",
    )(logits)


def _dispatch_kernel(s0_ref, s1_ref, h_ref, init_ref, xs_ref, sem, *, tm):
    del init_ref
    base = pl.program_id(0) * tm

    def row_copy(r, slot):
        return pltpu.make_async_copy(h_ref.at[pl.ds(pl.multiple_of(r * _ROW_TILE, _ROW_TILE), _ROW_TILE)],
                                     xs_ref.at[pl.ds(pl.multiple_of(slot * _ROW_TILE, _ROW_TILE), _ROW_TILE)],
                                     sem)

    def start(r, carry):
        row_copy(r, s0_ref[base + r]).start(priority=0)
        row_copy(r, s1_ref[base + r]).start(priority=1)
        return carry

    lax.fori_loop(0, tm, start, 0)

    def wait(r, carry):
        row_copy(r, s0_ref[base + r]).wait()
        row_copy(r, s1_ref[base + r]).wait()
        return carry

    lax.fori_loop(0, tm, wait, 0)


def _dispatch(h2, slot0, slot1, cap, t):
    n = h2.shape[0] // _ROW_TILE
    tm = min(512, t)
    grid_spec = pltpu.PrefetchScalarGridSpec(
        num_scalar_prefetch=2,
        grid=(n // tm,),
        in_specs=[pl.BlockSpec((tm * _ROW_TILE, LANES), lambda i, s0, s1: (i, 0)),
                  pl.BlockSpec(memory_space=pl.ANY)],
        out_specs=pl.BlockSpec(memory_space=pl.ANY),
        scratch_shapes=[pltpu.SemaphoreType.DMA(())])
    return pl.pallas_call(
        functools.partial(_dispatch_kernel, tm=tm),
        grid_spec=grid_spec,
        out_shape=jax.ShapeDtypeStruct((cap * _ROW_TILE, LANES), F32),
        input_output_aliases={3: 0},
        compiler_params=_cp(1),
        name="moe_dispatch",
    )(slot0, slot1, h2, jnp.zeros((cap * _ROW_TILE, LANES), F32))


def _combine_kernel(s0_ref, s1_ref, x_ref, wf_ref, gt_ref, ys_ref, o_ref, buf0, buf1, sem, *, tm):
    base = pl.program_id(0) * tm

    def row_copy(slot, buf, r):
        return pltpu.make_async_copy(ys_ref.at[pl.ds(pl.multiple_of(slot * _ROW_TILE, _ROW_TILE), _ROW_TILE)],
                                     buf.at[pl.ds(pl.multiple_of(r * _ROW_TILE, _ROW_TILE), _ROW_TILE)], sem)

    def start(r, carry):
        row_copy(s0_ref[base + r], buf0, r).start(priority=0)
        row_copy(s1_ref[base + r], buf1, r).start(priority=1)
        return carry

    lax.fori_loop(0, tm, start, 0)

    def wait(r, carry):
        row_copy(s0_ref[base + r], buf0, r).wait()
        row_copy(s1_ref[base + r], buf1, r).wait()
        return carry

    lax.fori_loop(0, tm, wait, 0)
    wf = wf_ref[...]
    f = wf[:, 0:1] * _load_token_tiles(buf0, tm) + wf[:, 1:2] * _load_token_tiles(buf1, tm)
    o_ref[...] = x_ref[...] + gt_ref[0] * f


def _combine(x2, ys, slot0, slot1, wf, gt, t):
    n = x2.shape[0]
    tm = min(512, t)
    tpb = t // tm
    row = pl.BlockSpec((tm, D_MODEL), lambda i, s0, s1: (i, 0))
    grid_spec = pltpu.PrefetchScalarGridSpec(
        num_scalar_prefetch=2,
        grid=(n // tm,),
        in_specs=[row, pl.BlockSpec((tm, LANES), lambda i, s0, s1: (i, 0)),
                  pl.BlockSpec((1, 1, D_MODEL), lambda i, s0, s1: (i // tpb, 0, 0)),
                  pl.BlockSpec(memory_space=pl.ANY)],
        out_specs=row,
        scratch_shapes=[pltpu.VMEM((tm * _ROW_TILE, LANES), F32), pltpu.VMEM((tm * _ROW_TILE, LANES), F32),
                        pltpu.SemaphoreType.DMA(())])
    return pl.pallas_call(
        functools.partial(_combine_kernel, tm=tm),
        grid_spec=grid_spec,
        out_shape=jax.ShapeDtypeStruct((n, D_MODEL), F32),
        compiler_params=_cp(1),
        name="moe_combine",
    )(slot0, slot1, x2, wf, gt, ys)


def _moe(h2, x2, logits, gt, wg, wu, wd, t):
    n = x2.shape[0]
    info_i, info_f, cnt = _route(logits, t)
    counts = cnt[0, :N_EXPERTS].astype(I32)
    padded = (counts + MOE_TILE - 1) // MOE_TILE * MOE_TILE
    pad_ends = jnp.cumsum(padded)
    pad_starts = pad_ends - padded
    slot0 = pad_starts[info_i[:, 0]] + info_i[:, 2]
    slot1 = pad_starts[info_i[:, 1]] + info_i[:, 3]
    cap = (2 * n // MOE_TILE + N_EXPERTS) * MOE_TILE
    n_tiles = cap // MOE_TILE
    n_used = (pad_ends[-1:] // MOE_TILE).astype(I32)
    tile_e = jnp.minimum(jnp.searchsorted(pad_ends, jnp.arange(n_tiles, dtype=I32) * MOE_TILE, side="right"),
                         N_EXPERTS - 1).astype(I32)
    last_e = tile_e[jnp.maximum(n_used[0] - 1, 0)]
    tile_e = jnp.where(jnp.arange(n_tiles) < n_used[0], tile_e, last_e)
    xs = _dispatch(h2, slot0, slot1, cap, t)
    ys = _moe_ffn(xs, tile_e, n_used, wg, wu, wd)
    return _combine(x2, ys, slot0, slot1, info_f, gt, t)


def _pad_last(a, width):
    return jnp.pad(a, [(0, 0)] * (a.ndim - 1) + [(0, width - a.shape[-1])])


def _swap_rope(a):
    h = MLA_ROPE // 2
    return jnp.concatenate([a[..., :MLA_NOPE], a[..., MLA_NOPE + h:], a[..., MLA_NOPE:MLA_NOPE + h]], axis=-1)


def _layer_params(l, w_in, mla_q_norm_g, mla_w_uq, mla_kv_norm_g, mla_w_ukv, qk_norm_q, qk_norm_k,
                  rwkv_mu, rwkv_w0, rwkv_w_up, rwkv_a0, rwkv_a_up, rwkv_g_up, rwkv_k_k, rwkv_k_a, rwkv_r_k):
    w = w_in[l]
    m0 = CONV_IN
    kr0 = m0 + MLA_Q_LORA + MLA_KV_LORA
    kr = w[:, kr0:kr0 + MLA_ROPE]
    h = MLA_ROPE // 2
    zeros = lambda c: jnp.zeros((D_MODEL, c), F32)
    kr_tile = jnp.concatenate([zeros(MLA_NOPE), kr, zeros(LANES - MLA_QK)], axis=1)
    krs_tile = jnp.concatenate([zeros(MLA_NOPE), kr[:, h:], kr[:, :h], zeros(LANES - MLA_QK)], axis=1)
    w_in_p = jnp.concatenate([w[:, :kr0], kr_tile, krs_tile, w[:, kr0 + MLA_ROPE:]], axis=1).astype(BF16)

    wq = mla_w_uq[l].reshape(MLA_Q_LORA, MLA_HEADS, MLA_QK)
    wkv = mla_w_ukv[l].reshape(MLA_KV_LORA, MLA_HEADS, MLA_NOPE + MLA_V)
    flat = lambda a: _pad_last(a, HEAD_PAD).reshape(a.shape[0], MLA_HEADS * HEAD_PAD).astype(BF16)
    mla = dict(
        gql=mla_q_norm_g[l].reshape(1, -1), gkvl=mla_kv_norm_g[l].reshape(1, -1),
        wq=flat(wq), wqs=flat(_swap_rope(wq)),
        wk=flat(wkv[..., :MLA_NOPE]), wv=flat(wkv[..., MLA_NOPE:]),
        gq=_pad_last(qk_norm_q[l], LANES).reshape(1, LANES),
        gqs=_pad_last(_swap_rope(qk_norm_q[l]), LANES).reshape(1, LANES),
        gk=_pad_last(qk_norm_k[l], LANES).reshape(1, LANES),
        gks=_pad_last(_swap_rope(qk_norm_k[l]), LANES).reshape(1, LANES))

    zl = jnp.zeros((LANES // 2, RWKV_CH), F32)
    hid = jnp.arange(RWKV_CH) // RWKV_HEAD
    rwkv = dict(
        mu=rwkv_mu[l].reshape(1, -1), w0=rwkv_w0[l].reshape(1, -1), a0=rwkv_a0[l].reshape(1, -1),
        wup=jnp.concatenate([rwkv_w_up[l], zl], axis=0).astype(BF16),
        aup=jnp.concatenate([zl, rwkv_a_up[l]], axis=0).astype(BF16),
        gup=rwkv_g_up[l].astype(BF16),
        kk=rwkv_k_k[l].reshape(1, -1), ka=rwkv_k_a[l].reshape(1, -1), rk=rwkv_r_k[l].reshape(1, -1),
        ones=(hid[:, None] == hid[None, :]).astype(BF16))
    return w_in_p, mla, rwkv


def kernel(x, c, positions, ada_w, ada_b, ada_layer_bias, norm1_g, norm2_g, w_in, w_out, conv_w, conv_b, conv_ln_g, conv_ln_b, mla_q_norm_g, mla_w_uq, mla_kv_norm_g, mla_w_ukv, qk_norm_q, qk_norm_k, rwkv_mu, rwkv_w0, rwkv_w_up, rwkv_a0, rwkv_a_up, rwkv_g_up, rwkv_k_k, rwkv_k_a, rwkv_r_k, rwkv_ln_g, rwkv_ln_b, ffn_w_gate, ffn_w_up, ffn_w_down, moe_router, moe_w_gate, moe_w_up, moe_w_down):
    b, t, d = x.shape
    n = b * t
    mods = _cond(c, ada_w, ada_b, ada_layer_bias)
    cos_t, sin_t = _rope_tables(positions)
    x2 = x.reshape(n, d)
    for l in range(DEPTH):
        sh1, sc1, gt1, sh2, sc2, gt2 = [mods[l, :, k * d:(k + 1) * d].reshape(b, 1, d) for k in range(N_MOD)]
        w_in_p, mla_p, rwkv_p = _layer_params(
            l, w_in, mla_q_norm_g, mla_w_uq, mla_kv_norm_g, mla_w_ukv, qk_norm_q, qk_norm_k,
            rwkv_mu, rwkv_w0, rwkv_w_up, rwkv_a0, rwkv_a_up, rwkv_g_up, rwkv_k_k, rwkv_k_a, rwkv_r_k)
        u_conv, u_mla, u_rwkv = _in_proj(x2, sc1, sh1, norm1_g[l].reshape(1, d), w_in_p, t)
        y_conv = _conformer_conv(u_conv.reshape(b, t, CONV_IN), conv_w[l], conv_b[l], conv_ln_g[l], conv_ln_b[l])
        q, kt, v = _mla_prep(u_mla.reshape(b, t, MLA_PAD_IN), cos_t, sin_t, mla_p)
        y_mla = _attention(q, kt, v)
        r_, lw_, k_, v_, a_, b_, bonus, gate = _rwkv_prep(u_rwkv.reshape(b, t, RWKV_IN), rwkv_p)
        qc, y0, g_, f_ = _rwkv_intra(r_, lw_, k_, v_, a_, b_)
        y_wkv = _rwkv_seq(qc, y0, g_, f_)
        i = l // 2
        router = None
        if l % 2 == 1:
            router = _pad_last(moe_router[i], LANES)
        flat = lambda a: a.reshape(n, a.shape[-1])
        outs = _out_proj(x2, flat(y_conv), flat(y_mla), flat(y_wkv), flat(bonus), flat(gate),
                         rwkv_ln_g[l].reshape(1, -1), rwkv_ln_b[l].reshape(1, -1), rwkv_p["ones"],
                         w_out[l].astype(BF16), gt1, norm2_g[l].reshape(1, d), sc2, sh2, router, t)
        if l % 2 == 0:
            x2, h2 = outs
            x2 = _dense_ffn(h2, x2, gt2, ffn_w_gate[i].astype(BF16), ffn_w_up[i].astype(BF16),
                            ffn_w_down[i].astype(BF16), t)
        else:
            x2, h2, logits = outs
            x2 = _moe(h2, x2, logits, gt2, moe_w_gate[i].astype(BF16), moe_w_up[i].astype(BF16),
                      moe_w_down[i].astype(BF16), t)
    return x2.reshape(b, t, d)
```

```python
import functools

import jax
import jax.numpy as jnp
from jax import lax
from jax.experimental import pallas as pl
from jax.experimental.pallas import tpu as pltpu

F32 = jnp.float32
BF16 = jnp.bfloat16
I32 = jnp.int32

D_MODEL = 1024
DEPTH = 4
N_MOD = 6
CONV_CH = 256
CONV_K = 31
CONV_IN = 2 * CONV_CH
MLA_NOPE = 64
MLA_ROPE = 32
MLA_V = 64
MLA_QK = MLA_NOPE + MLA_ROPE
MLA_HEADS = 8
MLA_OUT = MLA_HEADS * MLA_V
MLA_Q_LORA = 256
MLA_KV_LORA = 128
MLA_IN = MLA_Q_LORA + MLA_KV_LORA + MLA_ROPE
ROPE_THETA = 10000.0
RWKV_HEAD = 64
RWKV_CH = 256
RWKV_HEADS = 4
RWKV_IN = 1024
N_EXPERTS = 8
RMS_EPS = 1e-6
LN_EPS = 1e-5
RWKV_GN_EPS = 64e-5

LANES = 128
HEAD_PAD = LANES
MLA_PAD_IN = MLA_Q_LORA + MLA_KV_LORA + 2 * LANES
IN_PAD = CONV_IN + MLA_PAD_IN + RWKV_IN
CHUNK = 64
MOE_TILE = 512
_ROW_TILE = D_MODEL // LANES
VMEM_LIMIT = 56 * 1024 * 1024

_NEG = -1e30


def _cp(n_axes, vmem=None):
    return pltpu.CompilerParams(dimension_semantics=("arbitrary",) * n_axes,
                                vmem_limit_bytes=vmem if vmem else VMEM_LIMIT)


def _mm(a, b):
    return jnp.dot(a.astype(BF16), b.astype(BF16), preferred_element_type=F32)


def _mm_nt(a, b):
    return lax.dot_general(a.astype(BF16), b.astype(BF16), (((1,), (1,)), ((), ())),
                           preferred_element_type=F32)


def _mm_tn(a, b):
    return lax.dot_general(a.astype(BF16), b.astype(BF16), (((0,), (0,)), ((), ())),
                           preferred_element_type=F32)


def _sigmoid(x):
    return 1.0 / (1.0 + jnp.exp(-x))


def _split2(x):
    hi = x.astype(BF16)
    lo = (x - hi.astype(F32)).astype(BF16)
    return hi, lo


def _seg_sum(x, ones_bd):
    hi, lo = _split2(x)
    return (jnp.dot(hi, ones_bd, preferred_element_type=F32)
            + jnp.dot(lo, ones_bd, preferred_element_type=F32))


def _cond_kernel(c_ref, w_ref, b_ref, lb_ref, o_ref):
    c = c_ref[...]
    s = c * _sigmoid(c)
    base = jnp.dot(s, w_ref[...], precision=lax.Precision.HIGHEST, preferred_element_type=F32) + b_ref[...]
    for l in range(DEPTH):
        o_ref[l] = base + lb_ref[l]


def _cond(c, ada_w, ada_b, ada_layer_bias):
    b = c.shape[0]
    n = ada_w.shape[1]
    tn = 1536
    cp = jnp.zeros((8, D_MODEL), F32).at[:b].set(c)
    out = pl.pallas_call(
        _cond_kernel,
        grid=(n // tn,),
        in_specs=[pl.BlockSpec((8, D_MODEL), lambda j: (0, 0)),
                  pl.BlockSpec((D_MODEL, tn), lambda j: (0, j)),
                  pl.BlockSpec((1, tn), lambda j: (0, j)),
                  pl.BlockSpec((DEPTH, 1, tn), lambda j: (0, 0, j))],
        out_specs=pl.BlockSpec((DEPTH, 8, tn), lambda j: (0, 0, j)),
        out_shape=jax.ShapeDtypeStruct((DEPTH, 8, n), F32),
        compiler_params=_cp(1),
        name="cond",
    )(cp, ada_w, ada_b.reshape(1, n), ada_layer_bias.reshape(DEPTH, 1, n))
    return out[:, :b]


def _rope_kernel(pos_ref, invf_ref, c_ref, s_ref):
    pos = pos_ref[0].astype(F32)
    ang = pos * invf_ref[...]
    lane = lax.broadcasted_iota(I32, ang.shape, 1)
    cs = jnp.cos(ang)
    sn = jnp.sin(ang)
    c_ref[0] = jnp.where(lane < MLA_NOPE, 1.0, jnp.where(lane < MLA_QK, cs, 0.0))
    s_ref[0] = jnp.where((lane >= MLA_NOPE) & (lane < MLA_NOPE + 16), -sn,
                         jnp.where((lane >= MLA_NOPE + 16) & (lane < MLA_QK), sn, 0.0))


def _rope_tables(positions):
    b, t = positions.shape
    tt = min(512, t)
    inv_freq = ROPE_THETA ** (-jnp.arange(0, MLA_ROPE, 2, dtype=F32) / MLA_ROPE)
    invf = jnp.zeros((1, LANES), F32)
    invf = invf.at[0, MLA_NOPE:MLA_NOPE + 16].set(inv_freq).at[0, MLA_NOPE + 16:MLA_QK].set(inv_freq)
    spec = pl.BlockSpec((1, tt, LANES), lambda bi, i: (bi, i, 0))
    return pl.pallas_call(
        _rope_kernel,
        grid=(b, t // tt),
        in_specs=[pl.BlockSpec((1, tt, 1), lambda bi, i: (bi, i, 0)),
                  pl.BlockSpec((1, LANES), lambda bi, i: (0, 0))],
        out_specs=[spec, spec],
        out_shape=[jax.ShapeDtypeStruct((b, t, LANES), F32)] * 2,
        compiler_params=_cp(2),
        name="rope_tables",
    )(positions.reshape(b, t, 1), invf)


def _in_proj_kernel(x_ref, sc_ref, sh_ref, g_ref, w_ref, oc_ref, om_ref, or_ref):
    x = x_ref[...]
    ms = jnp.mean(x * x, axis=-1, keepdims=True)
    h = x * lax.rsqrt(ms + RMS_EPS) * g_ref[...]
    h = h * (1.0 + sc_ref[0]) + sh_ref[0]
    hb = h.astype(BF16)
    oc_ref[...] = jnp.dot(hb, w_ref[:, 0:CONV_IN], preferred_element_type=F32)
    om_ref[...] = jnp.dot(hb, w_ref[:, CONV_IN:CONV_IN + MLA_PAD_IN], preferred_element_type=F32)
    or_ref[...] = jnp.dot(hb, w_ref[:, CONV_IN + MLA_PAD_IN:IN_PAD], preferred_element_type=F32)


def _in_proj(x2, sc, sh, g, w, t):
    n = x2.shape[0]
    tm = min(512, t)
    tpb = t // tm
    vec = pl.BlockSpec((1, 1, D_MODEL), lambda i: (i // tpb, 0, 0))
    return pl.pallas_call(
        _in_proj_kernel,
        grid=(n // tm,),
        in_specs=[pl.BlockSpec((tm, D_MODEL), lambda i: (i, 0)), vec, vec,
                  pl.BlockSpec((1, D_MODEL), lambda i: (0, 0)),
                  pl.BlockSpec((D_MODEL, IN_PAD), lambda i: (0, 0))],
        out_specs=[pl.BlockSpec((tm, CONV_IN), lambda i: (i, 0)),
                   pl.BlockSpec((tm, MLA_PAD_IN), lambda i: (i, 0)),
                   pl.BlockSpec((tm, RWKV_IN), lambda i: (i, 0))],
        out_shape=[jax.ShapeDtypeStruct((n, CONV_IN), F32),
                   jax.ShapeDtypeStruct((n, MLA_PAD_IN), F32),
                   jax.ShapeDtypeStruct((n, RWKV_IN), F32)],
        compiler_params=_cp(1),
        name="in_proj",
    )(x2, sc, sh, g, w)


_CONV_HALO = 32


def _conv_kernel(u_ref, up_ref, w_ref, b_ref, g_ref, bb_ref, o_ref, zbuf, *, tt):
    i = pl.program_id(1)
    u = u_ref[0]
    z = u[:, :CONV_CH] * _sigmoid(u[:, CONV_CH:])
    up = up_ref[0]
    zp = up[:, :CONV_CH] * _sigmoid(up[:, CONV_CH:])
    zp = jnp.where(i > 0, zp, 0.0)
    zbuf[0:_CONV_HALO, :] = zp
    zbuf[_CONV_HALO:, :] = z
    acc = jnp.zeros((tt, CONV_CH), F32)
    for k in range(CONV_K):
        acc = acc + zbuf[pl.ds(_CONV_HALO - (CONV_K - 1) + k, tt), :] * w_ref[k:k + 1, :]
    y = acc + b_ref[...]
    mu = jnp.mean(y, axis=-1, keepdims=True)
    yc = y - mu
    var = jnp.mean(yc * yc, axis=-1, keepdims=True)
    yn = yc * lax.rsqrt(var + LN_EPS) * g_ref[...] + bb_ref[...]
    o_ref[0] = yn * _sigmoid(yn)


def _conformer_conv(u3, conv_w, conv_b, ln_g, ln_b):
    b, t, _ = u3.shape
    tt = min(512, t)
    hb = tt // _CONV_HALO
    vec = pl.BlockSpec((1, CONV_CH), lambda bi, i: (0, 0))
    return pl.pallas_call(
        functools.partial(_conv_kernel, tt=tt),
        grid=(b, t // tt),
        in_specs=[pl.BlockSpec((1, tt, CONV_IN), lambda bi, i: (bi, i, 0)),
                  pl.BlockSpec((1, _CONV_HALO, CONV_IN), lambda bi, i: (bi, jnp.maximum(i * hb - 1, 0), 0)),
                  pl.BlockSpec((CONV_K, CONV_CH), lambda bi, i: (0, 0)), vec, vec, vec],
        out_specs=pl.BlockSpec((1, tt, CONV_CH), lambda bi, i: (bi, i, 0)),
        out_shape=jax.ShapeDtypeStruct((b, t, CONV_CH), F32),
        scratch_shapes=[pltpu.VMEM((tt + _CONV_HALO, CONV_CH), F32)],
        compiler_params=_cp(2),
        name="conformer_conv",
    )(u3, u3, conv_w, conv_b.reshape(1, -1), ln_g.reshape(1, -1), ln_b.reshape(1, -1))


def _mla_prep_kernel(um_ref, c_ref, s_ref, gql_ref, wq_ref, wqs_ref, gkvl_ref, wk_ref, wv_ref,
                     gq_ref, gqs_ref, gk_ref, gks_ref, q_ref, kt_ref, v_ref):
    um = um_ref[0]
    cq = um[:, 0:MLA_Q_LORA]
    ckv = um[:, MLA_Q_LORA:MLA_Q_LORA + MLA_KV_LORA]
    kr = um[:, MLA_Q_LORA + MLA_KV_LORA:MLA_Q_LORA + MLA_KV_LORA + LANES]
    krs = um[:, MLA_Q_LORA + MLA_KV_LORA + LANES:MLA_PAD_IN]
    cqn = (cq * lax.rsqrt(jnp.mean(cq * cq, axis=-1, keepdims=True) + RMS_EPS) * gql_ref[...]).astype(BF16)
    ckvn = (ckv * lax.rsqrt(jnp.mean(ckv * ckv, axis=-1, keepdims=True) + RMS_EPS) * gkvl_ref[...]).astype(BF16)
    q = jnp.dot(cqn, wq_ref[...], preferred_element_type=F32)
    qs = jnp.dot(cqn, wqs_ref[...], preferred_element_type=F32)
    kn = jnp.dot(ckvn, wk_ref[...], preferred_element_type=F32)
    vv = jnp.dot(ckvn, wv_ref[...], preferred_element_type=F32)
    cos_t = c_ref[0]
    sin_t = s_ref[0]
    lane = lax.broadcasted_iota(I32, cos_t.shape, 1)
    scale = float(MLA_QK) ** -0.5 * _LOG2E
    inv_d = 1.0 / MLA_QK
    for h in range(MLA_HEADS):
        sl = slice(h * HEAD_PAD, (h + 1) * HEAD_PAD)
        qh = q[:, sl]
        r = lax.rsqrt(jnp.sum(qh * qh, axis=-1, keepdims=True) * inv_d + RMS_EPS)
        qf = qh * (r * gq_ref[...]) * cos_t + qs[:, sl] * (r * gqs_ref[...]) * sin_t
        q_ref[0, :, sl] = (qf * scale).astype(BF16)
        kh = kn[:, sl] + kr
        r = lax.rsqrt(jnp.sum(kh * kh, axis=-1, keepdims=True) * inv_d + RMS_EPS)
        kf = kh * (r * gk_ref[...]) * cos_t + krs * (r * gks_ref[...]) * sin_t
        kt_ref[0, h] = kf.T.astype(BF16)
        v_ref[0, h] = jnp.where(lane == MLA_V, 1.0, vv[:, sl]).astype(BF16)


def _mla_prep(um3, cos_t, sin_t, p):
    b, t, _ = um3.shape
    tt = min(256, t)
    full = lambda shape: pl.BlockSpec(shape, lambda bi, i: (0,) * len(shape))
    tab = pl.BlockSpec((1, tt, LANES), lambda bi, i: (bi, i, 0))
    return pl.pallas_call(
        _mla_prep_kernel,
        grid=(b, t // tt),
        in_specs=[pl.BlockSpec((1, tt, MLA_PAD_IN), lambda bi, i: (bi, i, 0)), tab, tab,
                  full((1, MLA_Q_LORA)), full((MLA_Q_LORA, MLA_HEADS * HEAD_PAD)),
                  full((MLA_Q_LORA, MLA_HEADS * HEAD_PAD)),
                  full((1, MLA_KV_LORA)), full((MLA_KV_LORA, MLA_HEADS * HEAD_PAD)),
                  full((MLA_KV_LORA, MLA_HEADS * HEAD_PAD)),
                  full((1, LANES)), full((1, LANES)), full((1, LANES)), full((1, LANES))],
        out_specs=[pl.BlockSpec((1, tt, MLA_HEADS * HEAD_PAD), lambda bi, i: (bi, i, 0)),
                   pl.BlockSpec((1, MLA_HEADS, HEAD_PAD, tt), lambda bi, i: (bi, 0, 0, i)),
                   pl.BlockSpec((1, MLA_HEADS, tt, HEAD_PAD), lambda bi, i: (bi, 0, i, 0))],
        out_shape=[jax.ShapeDtypeStruct((b, t, MLA_HEADS * HEAD_PAD), BF16),
                   jax.ShapeDtypeStruct((b, MLA_HEADS, HEAD_PAD, t), BF16),
                   jax.ShapeDtypeStruct((b, MLA_HEADS, t, HEAD_PAD), BF16)],
        compiler_params=_cp(2),
        name="mla_prep",
    )(um3, cos_t, sin_t, p["gql"], p["wq"], p["wqs"], p["gkvl"], p["wk"], p["wv"],
      p["gq"], p["gqs"], p["gk"], p["gks"])


_ATTN_HEAD_GROUP = 4
_ATTN_STRIP = 512
_LOG2E = 1.4426950408889634


def _attn_kernel(q_ref, kt_ref, v_ref, o_ref, *, tq, tk):
    i = pl.program_id(1)
    nfull = i * (tq // tk)
    row = lax.broadcasted_iota(I32, (tq, tk), 0)
    col = lax.broadcasted_iota(I32, (tq, tk), 1)
    outs = []

    def step(h, off, m, acc, mask):
        q = q_ref[0, :, h * HEAD_PAD:(h + 1) * HEAD_PAD]
        s = jnp.dot(q, kt_ref[0, h, :, pl.ds(off, tk)], preferred_element_type=F32)
        ms, ps, alphas = [], [], []
        for r0 in range(0, tq, _ATTN_STRIP):
            rows = slice(r0, r0 + _ATTN_STRIP)
            s_r = s[rows]
            if mask is not None:
                s_r = jnp.where(mask[rows], s_r, _NEG)
            m_old = m[rows]
            m_new = jnp.maximum(m_old, jnp.max(s_r, axis=-1, keepdims=True))
            ps.append(jnp.exp2(s_r - m_new).astype(BF16))
            alphas.append(jnp.exp2(m_old - m_new))
            ms.append(m_new)
        p = jnp.concatenate(ps, axis=0)
        acc = acc * jnp.concatenate(alphas, axis=0) + jnp.dot(p, v_ref[0, h, pl.ds(off, tk), :],
                                                              preferred_element_type=F32)
        return jnp.concatenate(ms, axis=0), acc

    for h0 in range(0, MLA_HEADS, _ATTN_HEAD_GROUP):
        heads = range(h0, h0 + _ATTN_HEAD_GROUP)

        def body(kb, carry):
            off = pl.multiple_of(kb * tk, tk)
            return tuple(step(h, off, carry[n][0], carry[n][1], None) for n, h in enumerate(heads))

        init = tuple((jnp.full((tq, 1), _NEG, F32), jnp.zeros((tq, HEAD_PAD), F32)) for _ in heads)
        carry = lax.fori_loop(0, nfull, body, init)
        for d in range(tq // tk):
            off = pl.multiple_of(i * tq + d * tk, tk)
            carry = tuple(step(h, off, carry[n][0], carry[n][1], col + d * tk <= row)
                          for n, h in enumerate(heads))
        for m, acc in carry:
            outs.append(acc[:, :MLA_V] / acc[:, MLA_V:MLA_V + 1])
    o_ref[0] = jnp.concatenate(outs, axis=-1)


def _attention(q, kt, v):
    b, t, _ = q.shape
    tq = min(512, t)
    tk = min(512, t)
    return pl.pallas_call(
        functools.partial(_attn_kernel, tq=tq, tk=tk),
        grid=(b, t // tq),
        in_specs=[pl.BlockSpec((1, tq, MLA_HEADS * HEAD_PAD), lambda bi, i: (bi, i, 0)),
                  pl.BlockSpec((1, MLA_HEADS, HEAD_PAD, t), lambda bi, i: (bi, 0, 0, 0)),
                  pl.BlockSpec((1, MLA_HEADS, t, HEAD_PAD), lambda bi, i: (bi, 0, 0, 0))],
        out_specs=pl.BlockSpec((1, tq, MLA_OUT), lambda bi, i: (bi, i, 0)),
        out_shape=jax.ShapeDtypeStruct((b, t, MLA_OUT), F32),
        compiler_params=_cp(2),
        name="mla_attention",
    )(q, kt, v)


def _rwkv_prep_kernel(u_ref, up_ref, mu_ref, w0_ref, wup_ref, a0_ref, aup_ref, gup_ref, kk_ref, ka_ref,
                      rk_ref, ones_ref, r_o, lw_o, k_o, v_o, a_o, b_o, bonus_o, g_o):
    i = pl.program_id(1)
    u = u_ref[0]
    prow = jnp.where(i > 0, up_ref[0][7:8, :], 0.0)
    rowi = lax.broadcasted_iota(I32, u.shape, 0)
    prev = jnp.where(rowi == 0, prow, pltpu.roll(u, 1, axis=0))
    uf = u + (prev - u) * mu_ref[...]
    c = RWKV_CH
    r = uf[:, 0:c]
    k = uf[:, c:2 * c]
    v = uf[:, 2 * c:3 * c]
    wa = uf[:, 3 * c:3 * c + LANES]
    gd = uf[:, 3 * c + LANES:]
    ones_bd = ones_ref[...]
    wl = w0_ref[...] + _mm(jnp.tanh(wa), wup_ref[...])
    z = -wl
    softplus = jnp.maximum(z, 0.0) + jnp.log(1.0 + jnp.exp(-jnp.abs(z)))
    lw = -jnp.exp(-softplus - 0.5)
    a_s = _sigmoid(a0_ref[...] + _mm(wa, aup_ref[...]))
    g = _mm(_sigmoid(gd), gup_ref[...])
    kk = k * kk_ref[...]
    nrm = jnp.maximum(jnp.sqrt(_seg_sum(kk * kk, ones_bd)), 1e-12)
    kkn = kk / nrm
    k2 = k * (1.0 + (a_s - 1.0) * ka_ref[...])
    r_o[0] = r
    lw_o[0] = lw
    k_o[0] = k2
    v_o[0] = v
    a_o[0] = -kkn
    b_o[0] = kkn * a_s
    bonus_o[0] = _seg_sum(r * k2 * rk_ref[...], ones_bd) * v
    g_o[0] = g


def _rwkv_prep(u3, p):
    b, t, _ = u3.shape
    tt = min(512, t)
    hb = tt // 8
    full = lambda shape: pl.BlockSpec(shape, lambda bi, i: (0,) * len(shape))
    out = pl.BlockSpec((1, tt, RWKV_CH), lambda bi, i: (bi, i, 0))
    return pl.pallas_call(
        _rwkv_prep_kernel,
        grid=(b, t // tt),
        in_specs=[pl.BlockSpec((1, tt, RWKV_IN), lambda bi, i: (bi, i, 0)),
                  pl.BlockSpec((1, 8, RWKV_IN), lambda bi, i: (bi, jnp.maximum(i * hb - 1, 0), 0)),
                  full((1, RWKV_IN)), full((1, RWKV_CH)), full((LANES, RWKV_CH)), full((1, RWKV_CH)),
                  full((LANES, RWKV_CH)), full((LANES, RWKV_CH)), full((1, RWKV_CH)), full((1, RWKV_CH)),
                  full((1, RWKV_CH)), full((RWKV_CH, RWKV_CH))],
        out_specs=[out] * 8,
        out_shape=[jax.ShapeDtypeStruct((b, t, RWKV_CH), F32)] * 8,
        compiler_params=_cp(2),
        name="rwkv_prep",
    )(u3, u3, p["mu"], p["w0"], p["wup"], p["a0"], p["aup"], p["gup"], p["kk"], p["ka"], p["rk"], p["ones"])


def _intra_chunks(rs, lws, ks, vs, as_, bs, mask_bd, eye_bd, tril, strict, incl, eye_p):
    c = CHUNK
    each = lambda fn, *lists: [fn(*xs) for xs in zip(*lists)]

    def bd(x):
        xb = x.astype(BF16)
        return jnp.where(mask_bd, jnp.concatenate([xb, xb, xb, xb], axis=0), jnp.zeros((), BF16))

    def prefix(lw):
        hi = lw.astype(BF16)
        rem = lw - hi.astype(F32)
        mid = rem.astype(BF16)
        lo = (rem - mid.astype(F32)).astype(BF16)
        return (jnp.dot(tril, hi, preferred_element_type=F32) + jnp.dot(tril, mid, preferred_element_type=F32)
                + jnp.dot(tril, lo, preferred_element_type=F32))

    cums = each(prefix, lws)
    tots = each(lambda cum: cum[c - 1:c, :], cums)
    e_negs = each(lambda cum: jnp.exp(-cum), cums)
    e_ends = each(lambda tot, cum: jnp.exp(tot - cum), tots, cums)
    rts = each(lambda r, cum: r * jnp.exp(cum), rs, cums)
    ats = each(lambda a, cum, lw: a * jnp.exp(cum - lw), as_, cums, lws)
    stacks = each(lambda at, rt: jnp.concatenate([at, rt], axis=0), ats, rts)
    s1s = each(lambda st, b, e: _mm_nt(st, bd(b * e)), stacks, bs, e_negs)
    s2s = each(lambda st, k, e: _mm_nt(st, bd(k * e)), stacks, ks, e_negs)
    a_abs = each(lambda s1: jnp.where(strict, s1[:c], 0.0), s1s)
    m_rbs = each(lambda s1: jnp.where(incl, s1[c:], 0.0), s1s)
    a_aks = each(lambda s2: jnp.where(strict, s2[:c], 0.0), s2s)
    m_rks = each(lambda s2: jnp.where(incl, s2[c:], 0.0), s2s)
    xs = each(lambda a_ab: eye_p + a_ab, a_abs)
    aks = each(lambda a_ab: _mm(a_ab, bd(a_ab)), a_abs)
    for j in range(5):
        if j < 4:
            pps = each(lambda ak, x: _mm(jnp.concatenate([ak, x], axis=0), bd(ak)), aks, xs)
            xs = each(lambda x, pp: x + pp[c:], xs, pps)
            aks = each(lambda pp: pp[:c], pps)
        else:
            xs = each(lambda x, ak: x + _mm(x, bd(ak)), xs, aks)
    bdvs = each(bd, vs)
    avs = each(_mm, a_aks, bdvs)
    ahs = each(lambda x, at: _mm(x, bd(at)), xs, ats)
    uhs = each(lambda x, av: _mm(x, bd(av)), xs, avs)
    qs = each(lambda rt, m_rb, ah: rt + _mm(m_rb, bd(ah)), rts, m_rbs, ahs)
    y0s = each(lambda m_rb, uh, m_rk, bdv: _mm(m_rb, bd(uh)) + _mm(m_rk, bdv), m_rbs, uhs, m_rks, bdvs)
    gs = each(lambda b, e, ah, tot: jnp.where(mask_bd, _mm_tn(b * e, ah), 0.0) + jnp.where(eye_bd, jnp.exp(tot), 0.0),
              bs, e_ends, ahs, tots)
    fs = each(lambda b, k, e, uh, v: jnp.where(mask_bd, _mm_tn(b * e, uh) + _mm_tn(k * e, v), 0.0),
              bs, ks, e_ends, uhs, vs)
    return qs, y0s, gs, fs


def _rwkv_intra_kernel(r_ref, lw_ref, k_ref, v_ref, a_ref, b_ref, q_o, y0_o, g_o, f_o, *, cps):
    c = CHUNK
    n = RWKV_CH
    ri = lax.broadcasted_iota(I32, (n, n), 0)
    ci = lax.broadcasted_iota(I32, (n, n), 1)
    mask_bd = (ri // RWKV_HEAD) == (ci // RWKV_HEAD)
    eye_bd = ri == ci
    t64 = lax.broadcasted_iota(I32, (c, c), 0)
    i64 = lax.broadcasted_iota(I32, (c, c), 1)
    tril = jnp.where(i64 <= t64, 1.0, 0.0).astype(BF16)
    tp = lax.broadcasted_iota(I32, (c, n), 0)
    ip = lax.broadcasted_iota(I32, (c, n), 1) & (RWKV_HEAD - 1)
    strict = ip < tp
    incl = ip <= tp
    eye_p = jnp.where(ip == tp, 1.0, 0.0)
    sls = [slice(s * c, (s + 1) * c) for s in range(cps)]
    load = lambda ref: [ref[0, sl, :] for sl in sls]
    qs, y0s, gs, fs = _intra_chunks(load(r_ref), load(lw_ref), load(k_ref), load(v_ref), load(a_ref), load(b_ref),
                                    mask_bd, eye_bd, tril, strict, incl, eye_p)
    for s, sl in enumerate(sls):
        q_o[0, sl, :] = qs[s]
        y0_o[0, sl, :] = y0s[s]
        g_o[0, s] = gs[s]
        f_o[0, s] = fs[s]


def _rwkv_intra(r, lw, k, v, a, b_):
    b, t, n = r.shape
    nc = t // CHUNK
    cps = max(d for d in (1, 2, 4, 8) if nc % d == 0)
    tok = pl.BlockSpec((1, cps * CHUNK, n), lambda bi, i: (bi, i, 0))
    mat = pl.BlockSpec((1, cps, n, n), lambda bi, i: (bi, i, 0, 0))
    return pl.pallas_call(
        functools.partial(_rwkv_intra_kernel, cps=cps),
        grid=(b, nc // cps),
        in_specs=[tok] * 6,
        out_specs=[tok, tok, mat, mat],
        out_shape=[jax.ShapeDtypeStruct((b, t, n), F32), jax.ShapeDtypeStruct((b, t, n), F32),
                   jax.ShapeDtypeStruct((b, nc, n, n), F32), jax.ShapeDtypeStruct((b, nc, n, n), F32)],
        compiler_params=_cp(2),
        name="rwkv_intra",
    )(r, lw, k, v, a, b_)


def _rwkv_seq_kernel(q_ref, y0_ref, g_ref, f_ref, o_ref, h_ref, *, nb):
    c = pl.program_id(0)

    @pl.when(c == 0)
    def _():
        h_ref[...] = jnp.zeros_like(h_ref)

    for b in range(nb):
        hb = h_ref[b].astype(BF16)
        o_ref[b] = jnp.dot(q_ref[b].astype(BF16), hb, preferred_element_type=F32) + y0_ref[b]
        h_ref[b] = jnp.dot(g_ref[b, 0].astype(BF16), hb, preferred_element_type=F32) + f_ref[b, 0]


def _rwkv_seq(q, y0, g, f):
    b, t, n = q.shape
    nc = t // CHUNK
    tok = pl.BlockSpec((b, CHUNK, n), lambda c: (0, c, 0))
    mat = pl.BlockSpec((b, 1, n, n), lambda c: (0, c, 0, 0))
    return pl.pallas_call(
        functools.partial(_rwkv_seq_kernel, nb=b),
        grid=(nc,),
        in_specs=[tok, tok, mat, mat],
        out_specs=tok,
        out_shape=jax.ShapeDtypeStruct((b, t, n), F32),
        scratch_shapes=[pltpu.VMEM((b, n, n), F32)],
        compiler_params=_cp(1),
        name="rwkv_seq",
    )(q, y0, g, f)


def _store_token_tiles(ref, val, tm):
    for p in range(D_MODEL // LANES):
        ref[pl.ds(p, tm, stride=_ROW_TILE), :] = val[:, p * LANES:(p + 1) * LANES]


def _load_token_tiles(ref, tm):
    return jnp.concatenate([ref[pl.ds(p, tm, stride=_ROW_TILE), :] for p in range(D_MODEL // LANES)], axis=-1)


def _out_proj_kernel(x_ref, yc_ref, ym_ref, yw_ref, bonus_ref, gate_ref, lng_ref, lnb_ref, ones_ref,
                     w_ref, gt_ref, g2_ref, sc_ref, sh_ref, *rest, with_router, tm):
    if with_router:
        wr_ref, xo_ref, h_ref, lg_ref = rest
    else:
        xo_ref, h_ref = rest
    ones_bd = ones_ref[...]
    inv = 1.0 / RWKV_HEAD
    y = yw_ref[...]
    mu = _seg_sum(y, ones_bd) * inv
    yc = y - mu
    var = _seg_sum(yc * yc, ones_bd) * inv
    yn = yc * lax.rsqrt(var + RWKV_GN_EPS) * lng_ref[...] + lnb_ref[...]
    yr = (yn + bonus_ref[...]) * gate_ref[...]
    mixed = (_mm(yc_ref[...], w_ref[0:CONV_CH, :])
             + _mm(ym_ref[...], w_ref[CONV_CH:CONV_CH + MLA_OUT, :])
             + _mm(yr, w_ref[CONV_CH + MLA_OUT:, :]))
    x = x_ref[...] + gt_ref[0] * mixed
    xo_ref[...] = x
    ms = jnp.mean(x * x, axis=-1, keepdims=True)
    h = x * lax.rsqrt(ms + RMS_EPS) * g2_ref[...]
    h = h * (1.0 + sc_ref[0]) + sh_ref[0]
    if with_router:
        _store_token_tiles(h_ref, h, tm)
    else:
        h_ref[...] = h
    if with_router:
        h_hi, h_lo = _split2(h)
        w_hi, w_lo = _split2(wr_ref[...])
        lg_ref[...] = (jnp.dot(h_hi, w_hi, preferred_element_type=F32)
                       + jnp.dot(h_lo, w_hi, preferred_element_type=F32)
                       + jnp.dot(h_hi, w_lo, preferred_element_type=F32))


def _out_proj(x2, yc, ym, yw, bonus, gate, ln_g, ln_b, ones_bd, w_out, gt, g2, sc, sh, router, t):
    n = x2.shape[0]
    tm = min(512, t)
    tpb = t // tm
    vec = pl.BlockSpec((1, 1, D_MODEL), lambda i: (i // tpb, 0, 0))
    row = lambda w: pl.BlockSpec((tm, w), lambda i: (i, 0))
    full = lambda r, c: pl.BlockSpec((r, c), lambda i: (0, 0))
    in_specs = [row(D_MODEL), row(CONV_CH), row(MLA_OUT), row(RWKV_CH), row(RWKV_CH), row(RWKV_CH),
                full(1, RWKV_CH), full(1, RWKV_CH), full(RWKV_CH, RWKV_CH),
                full(D_MODEL, D_MODEL), vec, full(1, D_MODEL), vec, vec]
    out_specs = [row(D_MODEL), row(D_MODEL)]
    out_shape = [jax.ShapeDtypeStruct((n, D_MODEL), F32), jax.ShapeDtypeStruct((n, D_MODEL), F32)]
    args = [x2, yc, ym, yw, bonus, gate, ln_g, ln_b, ones_bd, w_out, gt, g2, sc, sh]
    if router is not None:
        in_specs.append(full(D_MODEL, LANES))
        out_specs = [row(D_MODEL), pl.BlockSpec((tm * _ROW_TILE, LANES), lambda i: (i, 0)), row(LANES)]
        out_shape = [out_shape[0], jax.ShapeDtypeStruct((n * _ROW_TILE, LANES), F32),
                     jax.ShapeDtypeStruct((n, LANES), F32)]
        args.append(router)
    return pl.pallas_call(
        functools.partial(_out_proj_kernel, with_router=router is not None, tm=tm),
        grid=(n // tm,),
        in_specs=in_specs, out_specs=out_specs, out_shape=out_shape,
        compiler_params=_cp(1),
        name="out_proj",
    )(*args)


def _swiglu_step(load_h, wg_ref, wu_ref, wd_ref, hb_ref, acc_ref, j):
    @pl.when(j == 0)
    def _():
        hb_ref[...] = load_h().astype(BF16)
        acc_ref[...] = jnp.zeros_like(acc_ref)

    hb = hb_ref[...]
    gate = jnp.dot(hb, wg_ref[...], preferred_element_type=F32)
    up = jnp.dot(hb, wu_ref[...], preferred_element_type=F32)
    act = (gate * _sigmoid(gate) * up).astype(BF16)
    acc_ref[...] += jnp.dot(act, wd_ref[...], preferred_element_type=F32)


def _ffn_kernel(h_ref, x_ref, gt_ref, wg_ref, wu_ref, wd_ref, o_ref, hb_ref, acc_ref, *, nj):
    j = pl.program_id(1)
    _swiglu_step(lambda: h_ref[...], wg_ref, wu_ref, wd_ref, hb_ref, acc_ref, j)

    @pl.when(j == nj - 1)
    def _():
        o_ref[...] = x_ref[...] + gt_ref[0] * acc_ref[...]


def _dense_ffn(h2, x2, gt, wg, wu, wd, t):
    n = x2.shape[0]
    ff = wg.shape[1]
    tm = min(512, t)
    tpb = t // tm
    nj = 2
    tf = ff // nj
    row = pl.BlockSpec((tm, D_MODEL), lambda i, j: (i, 0))
    return pl.pallas_call(
        functools.partial(_ffn_kernel, nj=nj),
        grid=(n // tm, nj),
        in_specs=[row, row, pl.BlockSpec((1, 1, D_MODEL), lambda i, j: (i // tpb, 0, 0)),
                  pl.BlockSpec((D_MODEL, tf), lambda i, j: (0, j)),
                  pl.BlockSpec((D_MODEL, tf), lambda i, j: (0, j)),
                  pl.BlockSpec((tf, D_MODEL), lambda i, j: (j, 0))],
        out_specs=row,
        out_shape=jax.ShapeDtypeStruct((n, D_MODEL), F32),
        scratch_shapes=[pltpu.VMEM((tm, D_MODEL), BF16), pltpu.VMEM((tm, D_MODEL), F32)],
        compiler_params=_cp(2),
        name="dense_ffn",
    )(h2, x2, gt, wg, wu, wd)


def _moe_ffn_kernel(te_ref, nu_ref, xs_ref, wg_ref, wu_ref, wd_ref, o_ref, hb_ref, acc_ref, *, nj, tm):
    i = pl.program_id(0)
    j = pl.program_id(1)
    used = i < nu_ref[0]

    @pl.when(used)
    def _():
        _swiglu_step(lambda: _load_token_tiles(xs_ref, tm), wg_ref.at[0], wu_ref.at[0], wd_ref.at[0],
                     hb_ref, acc_ref, j)

    @pl.when(j == nj - 1)
    def _():
        _store_token_tiles(o_ref, jnp.where(used, acc_ref[...], 0.0), tm)


def _moe_ffn(xs, tile_e, n_used, wg, wu, wd):
    cap = xs.shape[0] // _ROW_TILE
    ff = wg.shape[2]
    nj = 2
    tf = ff // nj
    tm = MOE_TILE

    def wj(i, j, nu):
        return jnp.where(i < nu[0], j, nj - 1)

    row = pl.BlockSpec((tm * _ROW_TILE, LANES), lambda i, j, te, nu: (i, 0))
    grid_spec = pltpu.PrefetchScalarGridSpec(
        num_scalar_prefetch=2,
        grid=(cap // tm, nj),
        in_specs=[row,
                  pl.BlockSpec((1, D_MODEL, tf), lambda i, j, te, nu: (te[i], 0, wj(i, j, nu))),
                  pl.BlockSpec((1, D_MODEL, tf), lambda i, j, te, nu: (te[i], 0, wj(i, j, nu))),
                  pl.BlockSpec((1, tf, D_MODEL), lambda i, j, te, nu: (te[i], wj(i, j, nu), 0))],
        out_specs=row,
        scratch_shapes=[pltpu.VMEM((tm, D_MODEL), BF16), pltpu.VMEM((tm, D_MODEL), F32)])
    return pl.pallas_call(
        functools.partial(_moe_ffn_kernel, nj=nj, tm=tm),
        grid_spec=grid_spec,
        out_shape=jax.ShapeDtypeStruct((cap * _ROW_TILE, LANES), F32),
        compiler_params=_cp(2),
        name="moe_ffn",
    )(tile_e, n_used, xs, wg, wu, wd)


def _route_kernel(lg_ref, oi_ref, of_ref, cnt_ref, carry_ref, *, tm):
    i = pl.program_id(0)

    @pl.when(i == 0)
    def _():
        carry_ref[...] = jnp.zeros_like(carry_ref)

    lane = lax.broadcasted_iota(I32, (tm, LANES), 1)
    real = lane < N_EXPERTS
    lg = jnp.where(real, lg_ref[...], _NEG)
    ex = jnp.exp(lg - jnp.max(lg, axis=-1, keepdims=True))
    p = ex / jnp.sum(ex, axis=-1, keepdims=True)
    p = jnp.where(real, p, -1.0)
    m1 = jnp.max(p, axis=-1, keepdims=True)
    i1 = jnp.min(jnp.where(p == m1, lane, LANES), axis=-1, keepdims=True)
    p2 = jnp.where(lane == i1, -1.0, p)
    m2 = jnp.max(p2, axis=-1, keepdims=True)
    i2 = jnp.min(jnp.where(p2 == m2, lane, LANES), axis=-1, keepdims=True)
    den = m1 + m2
    oh1 = lane == i1
    oh2 = lane == i2
    oh = jnp.where(oh1 | oh2, 1.0, 0.0)
    rr = lax.broadcasted_iota(I32, (tm, tm), 0)
    cc = lax.broadcasted_iota(I32, (tm, tm), 1)
    lstrict = jnp.where(cc < rr, 1.0, 0.0).astype(BF16)
    pre = jnp.dot(lstrict, oh.astype(BF16), preferred_element_type=F32) + carry_ref[...]
    rank1 = jnp.sum(jnp.where(oh1, pre, 0.0), axis=-1, keepdims=True).astype(I32)
    rank2 = jnp.sum(jnp.where(oh2, pre, 0.0), axis=-1, keepdims=True).astype(I32)
    carry_ref[...] = carry_ref[...] + jnp.sum(oh, axis=0, keepdims=True)
    cnt_ref[...] = carry_ref[...]
    oi_ref[...] = jnp.where(lane == 0, i1, jnp.where(lane == 1, i2, jnp.where(lane == 2, rank1,
                                                                                jnp.where(lane == 3, rank2, 0))))
    of_ref[...] = jnp.where(lane == 0, m1 / den, jnp.where(lane == 1, m2 / den, 0.0))


def _route(logits, t):
    n = logits.shape[0]
    tm = min(512, t)
    row = pl.BlockSpec((tm, LANES), lambda i: (i, 0))
    return pl.pallas_call(
        functools.partial(_route_kernel, tm=tm),
        grid=(n // tm,),
        in_specs=[row],
        out_specs=[row, row, pl.BlockSpec((1, LANES), lambda i: (0, 0))],
        out_shape=[jax.ShapeDtypeStruct((n, LANES), I32), jax.ShapeDtypeStruct((n, LANES), F32),
                   jax.ShapeDtypeStruct((1, LANES), F32)],
        scratch_shapes=[pltpu.VMEM((1, LANES), F32)],
        compiler_params=_cp(1),
        name="moe_route",
    )(logits)


def _dispatch_kernel(s0_ref, s1_ref, h_ref, init_ref, xs_ref, sem, *, tm):
    del init_ref
    base = pl.program_id(0) * tm

    def row_copy(r, slot):
        return pltpu.make_async_copy(h_ref.at[pl.ds(pl.multiple_of(r * _ROW_TILE, _ROW_TILE), _ROW_TILE)],
                                     xs_ref.at[pl.ds(pl.multiple_of(slot * _ROW_TILE, _ROW_TILE), _ROW_TILE)],
                                     sem)

    def start(r, carry):
        row_copy(r, s0_ref[base + r]).start(priority=0)
        row_copy(r, s1_ref[base + r]).start(priority=1)
        return carry

    lax.fori_loop(0, tm, start, 0)

    def wait(r, carry):
        row_copy(r, s0_ref[base + r]).wait()
        row_copy(r, s1_ref[base + r]).wait()
        return carry

    lax.fori_loop(0, tm, wait, 0)


def _dispatch(h2, slot0, slot1, cap, t):
    n = h2.shape[0] // _ROW_TILE
    tm = min(512, t)
    grid_spec = pltpu.PrefetchScalarGridSpec(
        num_scalar_prefetch=2,
        grid=(n // tm,),
        in_specs=[pl.BlockSpec((tm * _ROW_TILE, LANES), lambda i, s0, s1: (i, 0)),
                  pl.BlockSpec(memory_space=pl.ANY)],
        out_specs=pl.BlockSpec(memory_space=pl.ANY),
        scratch_shapes=[pltpu.SemaphoreType.DMA(())])
    return pl.pallas_call(
        functools.partial(_dispatch_kernel, tm=tm),
        grid_spec=grid_spec,
        out_shape=jax.ShapeDtypeStruct((cap * _ROW_TILE, LANES), F32),
        input_output_aliases={3: 0},
        compiler_params=_cp(1),
        name="moe_dispatch",
    )(slot0, slot1, h2, jnp.zeros((cap * _ROW_TILE, LANES), F32))


def _combine_kernel(s0_ref, s1_ref, x_ref, wf_ref, gt_ref, ys_ref, o_ref, buf0, buf1, sem, *, tm):
    base = pl.program_id(0) * tm

    def row_copy(slot, buf, r):
        return pltpu.make_async_copy(ys_ref.at[pl.ds(pl.multiple_of(slot * _ROW_TILE, _ROW_TILE), _ROW_TILE)],
                                     buf.at[pl.ds(pl.multiple_of(r * _ROW_TILE, _ROW_TILE), _ROW_TILE)], sem)

    def start(r, carry):
        row_copy(s0_ref[base + r], buf0, r).start(priority=0)
        row_copy(s1_ref[base + r], buf1, r).start(priority=1)
        return carry

    lax.fori_loop(0, tm, start, 0)

    def wait(r, carry):
        row_copy(s0_ref[base + r], buf0, r).wait()
        row_copy(s1_ref[base + r], buf1, r).wait()
        return carry

    lax.fori_loop(0, tm, wait, 0)
    wf = wf_ref[...]
    f = wf[:, 0:1] * _load_token_tiles(buf0, tm) + wf[:, 1:2] * _load_token_tiles(buf1, tm)
    o_ref[...] = x_ref[...] + gt_ref[0] * f


def _combine(x2, ys, slot0, slot1, wf, gt, t):
    n = x2.shape[0]
    tm = min(512, t)
    tpb = t // tm
    row = pl.BlockSpec((tm, D_MODEL), lambda i, s0, s1: (i, 0))
    grid_spec = pltpu.PrefetchScalarGridSpec(
        num_scalar_prefetch=2,
        grid=(n // tm,),
        in_specs=[row, pl.BlockSpec((tm, LANES), lambda i, s0, s1: (i, 0)),
                  pl.BlockSpec((1, 1, D_MODEL), lambda i, s0, s1: (i // tpb, 0, 0)),
                  pl.BlockSpec(memory_space=pl.ANY)],
        out_specs=row,
        scratch_shapes=[pltpu.VMEM((tm * _ROW_TILE, LANES), F32), pltpu.VMEM((tm * _ROW_TILE, LANES), F32),
                        pltpu.SemaphoreType.DMA(())])
    return pl.pallas_call(
        functools.partial(_combine_kernel, tm=tm),
        grid_spec=grid_spec,
        out_shape=jax.ShapeDtypeStruct((n, D_MODEL), F32),
        compiler_params=_cp(1),
        name="moe_combine",
    )(slot0, slot1, x2, wf, gt, ys)


def _moe(h2, x2, logits, gt, wg, wu, wd, t):
    n = x2.shape[0]
    info_i, info_f, cnt = _route(logits, t)
    counts = cnt[0, :N_EXPERTS].astype(I32)
    padded = (counts + MOE_TILE - 1) // MOE_TILE * MOE_TILE
    pad_ends = jnp.cumsum(padded)
    pad_starts = pad_ends - padded
    slot0 = pad_starts[info_i[:, 0]] + info_i[:, 2]
    slot1 = pad_starts[info_i[:, 1]] + info_i[:, 3]
    cap = (2 * n // MOE_TILE + N_EXPERTS) * MOE_TILE
    n_tiles = cap // MOE_TILE
    n_used = (pad_ends[-1:] // MOE_TILE).astype(I32)
    tile_e = jnp.minimum(jnp.searchsorted(pad_ends, jnp.arange(n_tiles, dtype=I32) * MOE_TILE, side="right"),
                         N_EXPERTS - 1).astype(I32)
    last_e = tile_e[jnp.maximum(n_used[0] - 1, 0)]
    tile_e = jnp.where(jnp.arange(n_tiles) < n_used[0], tile_e, last_e)
    xs = _dispatch(h2, slot0, slot1, cap, t)
    ys = _moe_ffn(xs, tile_e, n_used, wg, wu, wd)
    return _combine(x2, ys, slot0, slot1, info_f, gt, t)


def _pad_last(a, width):
    return jnp.pad(a, [(0, 0)] * (a.ndim - 1) + [(0, width - a.shape[-1])])


def _swap_rope(a):
    h = MLA_ROPE // 2
    return jnp.concatenate([a[..., :MLA_NOPE], a[..., MLA_NOPE + h:], a[..., MLA_NOPE:MLA_NOPE + h]], axis=-1)


def _layer_params(l, w_in, mla_q_norm_g, mla_w_uq, mla_kv_norm_g, mla_w_ukv, qk_norm_q, qk_norm_k,
                  rwkv_mu, rwkv_w0, rwkv_w_up, rwkv_a0, rwkv_a_up, rwkv_g_up, rwkv_k_k, rwkv_k_a, rwkv_r_k):
    w = w_in[l]
    m0 = CONV_IN
    kr0 = m0 + MLA_Q_LORA + MLA_KV_LORA
    kr = w[:, kr0:kr0 + MLA_ROPE]
    h = MLA_ROPE // 2
    zeros = lambda c: jnp.zeros((D_MODEL, c), F32)
    kr_tile = jnp.concatenate([zeros(MLA_NOPE), kr, zeros(LANES - MLA_QK)], axis=1)
    krs_tile = jnp.concatenate([zeros(MLA_NOPE), kr[:, h:], kr[:, :h], zeros(LANES - MLA_QK)], axis=1)
    w_in_p = jnp.concatenate([w[:, :kr0], kr_tile, krs_tile, w[:, kr0 + MLA_ROPE:]], axis=1).astype(BF16)

    wq = mla_w_uq[l].reshape(MLA_Q_LORA, MLA_HEADS, MLA_QK)
    wkv = mla_w_ukv[l].reshape(MLA_KV_LORA, MLA_HEADS, MLA_NOPE + MLA_V)
    flat = lambda a: _pad_last(a, HEAD_PAD).reshape(a.shape[0], MLA_HEADS * HEAD_PAD).astype(BF16)
    mla = dict(
        gql=mla_q_norm_g[l].reshape(1, -1), gkvl=mla_kv_norm_g[l].reshape(1, -1),
        wq=flat(wq), wqs=flat(_swap_rope(wq)),
        wk=flat(wkv[..., :MLA_NOPE]), wv=flat(wkv[..., MLA_NOPE:]),
        gq=_pad_last(qk_norm_q[l], LANES).reshape(1, LANES),
        gqs=_pad_last(_swap_rope(qk_norm_q[l]), LANES).reshape(1, LANES),
        gk=_pad_last(qk_norm_k[l], LANES).reshape(1, LANES),
        gks=_pad_last(_swap_rope(qk_norm_k[l]), LANES).reshape(1, LANES))

    zl = jnp.zeros((LANES // 2, RWKV_CH), F32)
    hid = jnp.arange(RWKV_CH) // RWKV_HEAD
    rwkv = dict(
        mu=rwkv_mu[l].reshape(1, -1), w0=rwkv_w0[l].reshape(1, -1), a0=rwkv_a0[l].reshape(1, -1),
        wup=jnp.concatenate([rwkv_w_up[l], zl], axis=0).astype(BF16),
        aup=jnp.concatenate([zl, rwkv_a_up[l]], axis=0).astype(BF16),
        gup=rwkv_g_up[l].astype(BF16),
        kk=rwkv_k_k[l].reshape(1, -1), ka=rwkv_k_a[l].reshape(1, -1), rk=rwkv_r_k[l].reshape(1, -1),
        ones=(hid[:, None] == hid[None, :]).astype(BF16))
    return w_in_p, mla, rwkv


def kernel(x, c, positions, ada_w, ada_b, ada_layer_bias, norm1_g, norm2_g, w_in, w_out, conv_w, conv_b, conv_ln_g, conv_ln_b, mla_q_norm_g, mla_w_uq, mla_kv_norm_g, mla_w_ukv, qk_norm_q, qk_norm_k, rwkv_mu, rwkv_w0, rwkv_w_up, rwkv_a0, rwkv_a_up, rwkv_g_up, rwkv_k_k, rwkv_k_a, rwkv_r_k, rwkv_ln_g, rwkv_ln_b, ffn_w_gate, ffn_w_up, ffn_w_down, moe_router, moe_w_gate, moe_w_up, moe_w_down):
    b, t, d = x.shape
    n = b * t
    mods = _cond(c, ada_w, ada_b, ada_layer_bias)
    cos_t, sin_t = _rope_tables(positions)
    x2 = x.reshape(n, d)
    for l in range(DEPTH):
        sh1, sc1, gt1, sh2, sc2, gt2 = [mods[l, :, k * d:(k + 1) * d].reshape(b, 1, d) for k in range(N_MOD)]
        w_in_p, mla_p, rwkv_p = _layer_params(
            l, w_in, mla_q_norm_g, mla_w_uq, mla_kv_norm_g, mla_w_ukv, qk_norm_q, qk_norm_k,
            rwkv_mu, rwkv_w0, rwkv_w_up, rwkv_a0, rwkv_a_up, rwkv_g_up, rwkv_k_k, rwkv_k_a, rwkv_r_k)
        u_conv, u_mla, u_rwkv = _in_proj(x2, sc1, sh1, norm1_g[l].reshape(1, d), w_in_p, t)
        y_conv = _conformer_conv(u_conv.reshape(b, t, CONV_IN), conv_w[l], conv_b[l], conv_ln_g[l], conv_ln_b[l])
        q, kt, v = _mla_prep(u_mla.reshape(b, t, MLA_PAD_IN), cos_t, sin_t, mla_p)
        y_mla = _attention(q, kt, v)
        r_, lw_, k_, v_, a_, b_, bonus, gate = _rwkv_prep(u_rwkv.reshape(b, t, RWKV_IN), rwkv_p)
        qc, y0, g_, f_ = _rwkv_intra(r_, lw_, k_, v_, a_, b_)
        y_wkv = _rwkv_seq(qc, y0, g_, f_)
        i = l // 2
        router = None
        if l % 2 == 1:
            router = _pad_last(moe_router[i], LANES)
        flat = lambda a: a.reshape(n, a.shape[-1])
        outs = _out_proj(x2, flat(y_conv), flat(y_mla), flat(y_wkv), flat(bonus), flat(gate),
                         rwkv_ln_g[l].reshape(1, -1), rwkv_ln_b[l].reshape(1, -1), rwkv_p["ones"],
                         w_out[l].astype(BF16), gt1, norm2_g[l].reshape(1, d), sc2, sh2, router, t)
        if l % 2 == 0:
            x2, h2 = outs
            x2 = _dense_ffn(h2, x2, gt2, ffn_w_gate[i].astype(BF16), ffn_w_up[i].astype(BF16),
                            ffn_w_down[i].astype(BF16), t)
        else:
            x2, h2, logits = outs
            x2 = _moe(h2, x2, logits, gt2, moe_w_gate[i].astype(BF16), moe_w_up[i].astype(BF16),
                      moe_w_down[i].astype(BF16), t)
    return x2.reshape(b, t, d)
```

```python
import functools

import jax
import jax.numpy as jnp
from jax import lax
from jax.experimental import pallas as pl
from jax.experimental.pallas import tpu as pltpu

F32 = jnp.float32
BF16 = jnp.bfloat16
I32 = jnp.int32

D_MODEL = 1024
DEPTH = 4
N_MOD = 6
CONV_CH = 256
CONV_K = 31
CONV_IN = 2 * CONV_CH
MLA_NOPE = 64
MLA_ROPE = 32
MLA_V = 64
MLA_QK = MLA_NOPE + MLA_ROPE
MLA_HEADS = 8
MLA_OUT = MLA_HEADS * MLA_V
MLA_Q_LORA = 256
MLA_KV_LORA = 128
MLA_IN = MLA_Q_LORA + MLA_KV_LORA + MLA_ROPE
ROPE_THETA = 10000.0
RWKV_HEAD = 64
RWKV_CH = 256
RWKV_HEADS = 4
RWKV_IN = 1024
N_EXPERTS = 8
RMS_EPS = 1e-6
LN_EPS = 1e-5
RWKV_GN_EPS = 64e-5

LANES = 128
HEAD_PAD = LANES
MLA_PAD_IN = MLA_Q_LORA + MLA_KV_LORA + 2 * LANES
IN_PAD = CONV_IN + MLA_PAD_IN + RWKV_IN
CHUNK = 64
MOE_TILE = 512
_ROW_TILE = D_MODEL // LANES
VMEM_LIMIT = 56 * 1024 * 1024

_NEG = -1e30


def _cp(n_axes, vmem=None):
    return pltpu.CompilerParams(dimension_semantics=("arbitrary",) * n_axes,
                                vmem_limit_bytes=vmem if vmem else VMEM_LIMIT)


def _mm(a, b):
    return jnp.dot(a.astype(BF16), b.astype(BF16), preferred_element_type=F32)


def _mm_nt(a, b):
    return lax.dot_general(a.astype(BF16), b.astype(BF16), (((1,), (1,)), ((), ())),
                           preferred_element_type=F32)


def _mm_tn(a, b):
    return lax.dot_general(a.astype(BF16), b.astype(BF16), (((0,), (0,)), ((), ())),
                           preferred_element_type=F32)


def _sigmoid(x):
    return 1.0 / (1.0 + jnp.exp(-x))


def _split2(x):
    hi = x.astype(BF16)
    lo = (x - hi.astype(F32)).astype(BF16)
    return hi, lo


def _seg_sum(x, ones_bd):
    hi, lo = _split2(x)
    return (jnp.dot(hi, ones_bd, preferred_element_type=F32)
            + jnp.dot(lo, ones_bd, preferred_element_type=F32))


def _cond_kernel(c_ref, w_ref, b_ref, lb_ref, o_ref):
    c = c_ref[...]
    s = c * _sigmoid(c)
    base = jnp.dot(s, w_ref[...], precision=lax.Precision.HIGHEST, preferred_element_type=F32) + b_ref[...]
    for l in range(DEPTH):
        o_ref[l] = base + lb_ref[l]


def _cond(c, ada_w, ada_b, ada_layer_bias):
    b = c.shape[0]
    n = ada_w.shape[1]
    tn = 1536
    cp = jnp.zeros((8, D_MODEL), F32).at[:b].set(c)
    out = pl.pallas_call(
        _cond_kernel,
        grid=(n // tn,),
        in_specs=[pl.BlockSpec((8, D_MODEL), lambda j: (0, 0)),
                  pl.BlockSpec((D_MODEL, tn), lambda j: (0, j)),
                  pl.BlockSpec((1, tn), lambda j: (0, j)),
                  pl.BlockSpec((DEPTH, 1, tn), lambda j: (0, 0, j))],
        out_specs=pl.BlockSpec((DEPTH, 8, tn), lambda j: (0, 0, j)),
        out_shape=jax.ShapeDtypeStruct((DEPTH, 8, n), F32),
        compiler_params=_cp(1),
        name="cond",
    )(cp, ada_w, ada_b.reshape(1, n), ada_layer_bias.reshape(DEPTH, 1, n))
    return out[:, :b]


def _rope_kernel(pos_ref, invf_ref, c_ref, s_ref):
    pos = pos_ref[0].astype(F32)
    ang = pos * invf_ref[...]
    lane = lax.broadcasted_iota(I32, ang.shape, 1)
    cs = jnp.cos(ang)
    sn = jnp.sin(ang)
    c_ref[0] = jnp.where(lane < MLA_NOPE, 1.0, jnp.where(lane < MLA_QK, cs, 0.0))
    s_ref[0] = jnp.where((lane >= MLA_NOPE) & (lane < MLA_NOPE + 16), -sn,
                         jnp.where((lane >= MLA_NOPE + 16) & (lane < MLA_QK), sn, 0.0))


def _rope_tables(positions):
    b, t = positions.shape
    tt = min(512, t)
    inv_freq = ROPE_THETA ** (-jnp.arange(0, MLA_ROPE, 2, dtype=F32) / MLA_ROPE)
    invf = jnp.zeros((1, LANES), F32)
    invf = invf.at[0, MLA_NOPE:MLA_NOPE + 16].set(inv_freq).at[0, MLA_NOPE + 16:MLA_QK].set(inv_freq)
    spec = pl.BlockSpec((1, tt, LANES), lambda bi, i: (bi, i, 0))
    return pl.pallas_call(
        _rope_kernel,
        grid=(b, t // tt),
        in_specs=[pl.BlockSpec((1, tt, 1), lambda bi, i: (bi, i, 0)),
                  pl.BlockSpec((1, LANES), lambda bi, i: (0, 0))],
        out_specs=[spec, spec],
        out_shape=[jax.ShapeDtypeStruct((b, t, LANES), F32)] * 2,
        compiler_params=_cp(2),
        name="rope_tables",
    )(positions.reshape(b, t, 1), invf)


def _in_proj_kernel(x_ref, sc_ref, sh_ref, g_ref, w_ref, oc_ref, om_ref, or_ref):
    x = x_ref[...]
    ms = jnp.mean(x * x, axis=-1, keepdims=True)
    h = x * lax.rsqrt(ms + RMS_EPS) * g_ref[...]
    h = h * (1.0 + sc_ref[0]) + sh_ref[0]
    hb = h.astype(BF16)
    oc_ref[...] = jnp.dot(hb, w_ref[:, 0:CONV_IN], preferred_element_type=F32)
    om_ref[...] = jnp.dot(hb, w_ref[:, CONV_IN:CONV_IN + MLA_PAD_IN], preferred_element_type=F32)
    or_ref[...] = jnp.dot(hb, w_ref[:, CONV_IN + MLA_PAD_IN:IN_PAD], preferred_element_type=F32)


def _in_proj(x2, sc, sh, g, w, t):
    n = x2.shape[0]
    tm = min(512, t)
    tpb = t // tm
    vec = pl.BlockSpec((1, 1, D_MODEL), lambda i: (i // tpb, 0, 0))
    return pl.pallas_call(
        _in_proj_kernel,
        grid=(n // tm,),
        in_specs=[pl.BlockSpec((tm, D_MODEL), lambda i: (i, 0)), vec, vec,
                  pl.BlockSpec((1, D_MODEL), lambda i: (0, 0)),
                  pl.BlockSpec((D_MODEL, IN_PAD), lambda i: (0, 0))],
        out_specs=[pl.BlockSpec((tm, CONV_IN), lambda i: (i, 0)),
                   pl.BlockSpec((tm, MLA_PAD_IN), lambda i: (i, 0)),
                   pl.BlockSpec((tm, RWKV_IN), lambda i: (i, 0))],
        out_shape=[jax.ShapeDtypeStruct((n, CONV_IN), F32),
                   jax.ShapeDtypeStruct((n, MLA_PAD_IN), F32),
                   jax.ShapeDtypeStruct((n, RWKV_IN), F32)],
        compiler_params=_cp(1),
        name="in_proj",
    )(x2, sc, sh, g, w)


_CONV_HALO = 32
_SUBLANES = 8


def _conv_kernel(u_ref, up_ref, w_ref, b_ref, g_ref, bb_ref, o_ref, zbuf, *, tt):
    i = pl.program_id(1)
    u = u_ref[0]
    z = u[:, :CONV_CH] * _sigmoid(u[:, CONV_CH:])
    up = up_ref[0]
    zp = up[:, :CONV_CH] * _sigmoid(up[:, CONV_CH:])
    zp = jnp.where(i > 0, zp, 0.0)
    zbuf[0:_CONV_HALO, :] = zp
    zbuf[_CONV_HALO:, :] = z
    acc = jnp.zeros((tt, CONV_CH), F32)
    first = _CONV_HALO - (CONV_K - 1)
    zfull = zbuf[...]
    nrows = tt + _CONV_HALO
    for sh in range(_SUBLANES):
        taps = [k for k in range(CONV_K) if (k + first) % _SUBLANES == sh]
        shifted = zfull if sh == 0 else pltpu.roll(zfull, nrows - sh, axis=0)
        for k in taps:
            a0 = k + first - sh
            acc = acc + shifted[a0:a0 + tt] * w_ref[k:k + 1, :]
    y = acc + b_ref[...]
    mu = jnp.mean(y, axis=-1, keepdims=True)
    yc = y - mu
    var = jnp.mean(yc * yc, axis=-1, keepdims=True)
    yn = yc * lax.rsqrt(var + LN_EPS) * g_ref[...] + bb_ref[...]
    o_ref[0] = yn * _sigmoid(yn)


def _conformer_conv(u3, conv_w, conv_b, ln_g, ln_b):
    b, t, _ = u3.shape
    tt = min(512, t)
    hb = tt // _CONV_HALO
    vec = pl.BlockSpec((1, CONV_CH), lambda bi, i: (0, 0))
    return pl.pallas_call(
        functools.partial(_conv_kernel, tt=tt),
        grid=(b, t // tt),
        in_specs=[pl.BlockSpec((1, tt, CONV_IN), lambda bi, i: (bi, i, 0)),
                  pl.BlockSpec((1, _CONV_HALO, CONV_IN), lambda bi, i: (bi, jnp.maximum(i * hb - 1, 0), 0)),
                  pl.BlockSpec((CONV_K, CONV_CH), lambda bi, i: (0, 0)), vec, vec, vec],
        out_specs=pl.BlockSpec((1, tt, CONV_CH), lambda bi, i: (bi, i, 0)),
        out_shape=jax.ShapeDtypeStruct((b, t, CONV_CH), F32),
        scratch_shapes=[pltpu.VMEM((tt + _CONV_HALO, CONV_CH), F32)],
        compiler_params=_cp(2),
        name="conformer_conv",
    )(u3, u3, conv_w, conv_b.reshape(1, -1), ln_g.reshape(1, -1), ln_b.reshape(1, -1))


def _mla_prep_kernel(um_ref, c_ref, s_ref, gql_ref, wq_ref, wqs_ref, gkvl_ref, wk_ref, wv_ref,
                     gq_ref, gqs_ref, gk_ref, gks_ref, q_ref, kt_ref, v_ref):
    um = um_ref[0]
    cq = um[:, 0:MLA_Q_LORA]
    ckv = um[:, MLA_Q_LORA:MLA_Q_LORA + MLA_KV_LORA]
    kr = um[:, MLA_Q_LORA + MLA_KV_LORA:MLA_Q_LORA + MLA_KV_LORA + LANES]
    krs = um[:, MLA_Q_LORA + MLA_KV_LORA + LANES:MLA_PAD_IN]
    cqn = (cq * lax.rsqrt(jnp.mean(cq * cq, axis=-1, keepdims=True) + RMS_EPS) * gql_ref[...]).astype(BF16)
    ckvn = (ckv * lax.rsqrt(jnp.mean(ckv * ckv, axis=-1, keepdims=True) + RMS_EPS) * gkvl_ref[...]).astype(BF16)
    q = jnp.dot(cqn, wq_ref[...], preferred_element_type=F32)
    qs = jnp.dot(cqn, wqs_ref[...], preferred_element_type=F32)
    kn = jnp.dot(ckvn, wk_ref[...], preferred_element_type=F32)
    vv = jnp.dot(ckvn, wv_ref[...], preferred_element_type=F32)
    cos_t = c_ref[0]
    sin_t = s_ref[0]
    lane = lax.broadcasted_iota(I32, cos_t.shape, 1)
    scale = float(MLA_QK) ** -0.5 * _LOG2E
    inv_d = 1.0 / MLA_QK
    for h in range(MLA_HEADS):
        sl = slice(h * HEAD_PAD, (h + 1) * HEAD_PAD)
        qh = q[:, sl]
        r = lax.rsqrt(jnp.sum(qh * qh, axis=-1, keepdims=True) * inv_d + RMS_EPS)
        qf = qh * (r * gq_ref[...]) * cos_t + qs[:, sl] * (r * gqs_ref[...]) * sin_t
        q_ref[0, :, sl] = (qf * scale).astype(BF16)
        kh = kn[:, sl] + kr
        r = lax.rsqrt(jnp.sum(kh * kh, axis=-1, keepdims=True) * inv_d + RMS_EPS)
        kf = kh * (r * gk_ref[...]) * cos_t + krs * (r * gks_ref[...]) * sin_t
        kt_ref[0, h] = kf.T.astype(BF16)
        v_ref[0, h] = jnp.where(lane == MLA_V, 1.0, vv[:, sl]).astype(BF16)


def _mla_prep(um3, cos_t, sin_t, p):
    b, t, _ = um3.shape
    tt = min(256, t)
    full = lambda shape: pl.BlockSpec(shape, lambda bi, i: (0,) * len(shape))
    tab = pl.BlockSpec((1, tt, LANES), lambda bi, i: (bi, i, 0))
    return pl.pallas_call(
        _mla_prep_kernel,
        grid=(b, t // tt),
        in_specs=[pl.BlockSpec((1, tt, MLA_PAD_IN), lambda bi, i: (bi, i, 0)), tab, tab,
                  full((1, MLA_Q_LORA)), full((MLA_Q_LORA, MLA_HEADS * HEAD_PAD)),
                  full((MLA_Q_LORA, MLA_HEADS * HEAD_PAD)),
                  full((1, MLA_KV_LORA)), full((MLA_KV_LORA, MLA_HEADS * HEAD_PAD)),
                  full((MLA_KV_LORA, MLA_HEADS * HEAD_PAD)),
                  full((1, LANES)), full((1, LANES)), full((1, LANES)), full((1, LANES))],
        out_specs=[pl.BlockSpec((1, tt, MLA_HEADS * HEAD_PAD), lambda bi, i: (bi, i, 0)),
                   pl.BlockSpec((1, MLA_HEADS, HEAD_PAD, tt), lambda bi, i: (bi, 0, 0, i)),
                   pl.BlockSpec((1, MLA_HEADS, tt, HEAD_PAD), lambda bi, i: (bi, 0, i, 0))],
        out_shape=[jax.ShapeDtypeStruct((b, t, MLA_HEADS * HEAD_PAD), BF16),
                   jax.ShapeDtypeStruct((b, MLA_HEADS, HEAD_PAD, t), BF16),
                   jax.ShapeDtypeStruct((b, MLA_HEADS, t, HEAD_PAD), BF16)],
        compiler_params=_cp(2),
        name="mla_prep",
    )(um3, cos_t, sin_t, p["gql"], p["wq"], p["wqs"], p["gkvl"], p["wk"], p["wv"],
      p["gq"], p["gqs"], p["gk"], p["gks"])


_ATTN_HEAD_GROUP = 4
_ATTN_STRIP = 512
_LOG2E = 1.4426950408889634


def _attn_kernel(q_ref, kt_ref, v_ref, o_ref, *, tq, tk):
    i = pl.program_id(1)
    nfull = i * (tq // tk)
    row = lax.broadcasted_iota(I32, (tq, tk), 0)
    col = lax.broadcasted_iota(I32, (tq, tk), 1)
    outs = []

    def step(h, off, m, acc, mask):
        q = q_ref[0, :, h * HEAD_PAD:(h + 1) * HEAD_PAD]
        s = jnp.dot(q, kt_ref[0, h, :, pl.ds(off, tk)], preferred_element_type=F32)
        ms, ps, alphas = [], [], []
        for r0 in range(0, tq, _ATTN_STRIP):
            rows = slice(r0, r0 + _ATTN_STRIP)
            s_r = s[rows]
            if mask is not None:
                s_r = jnp.where(mask[rows], s_r, _NEG)
            m_old = m[rows]
            m_new = jnp.maximum(m_old, jnp.max(s_r, axis=-1, keepdims=True))
            ps.append(jnp.exp2(s_r - m_new).astype(BF16))
            alphas.append(jnp.exp2(m_old - m_new))
            ms.append(m_new)
        p = jnp.concatenate(ps, axis=0)
        acc = acc * jnp.concatenate(alphas, axis=0) + jnp.dot(p, v_ref[0, h, pl.ds(off, tk), :],
                                                              preferred_element_type=F32)
        return jnp.concatenate(ms, axis=0), acc

    for h0 in range(0, MLA_HEADS, _ATTN_HEAD_GROUP):
        heads = range(h0, h0 + _ATTN_HEAD_GROUP)

        def body(kb, carry):
            off = pl.multiple_of(kb * tk, tk)
            return tuple(step(h, off, carry[n][0], carry[n][1], None) for n, h in enumerate(heads))

        init = tuple((jnp.full((tq, 1), _NEG, F32), jnp.zeros((tq, HEAD_PAD), F32)) for _ in heads)
        carry = lax.fori_loop(0, nfull, body, init)
        for d in range(tq // tk):
            off = pl.multiple_of(i * tq + d * tk, tk)
            carry = tuple(step(h, off, carry[n][0], carry[n][1], col + d * tk <= row)
                          for n, h in enumerate(heads))
        for m, acc in carry:
            outs.append(acc[:, :MLA_V] / acc[:, MLA_V:MLA_V + 1])
    o_ref[0] = jnp.concatenate(outs, axis=-1)


def _attention(q, kt, v):
    b, t, _ = q.shape
    tq = min(512, t)
    tk = min(512, t)
    return pl.pallas_call(
        functools.partial(_attn_kernel, tq=tq, tk=tk),
        grid=(b, t // tq),
        in_specs=[pl.BlockSpec((1, tq, MLA_HEADS * HEAD_PAD), lambda bi, i: (bi, i, 0)),
                  pl.BlockSpec((1, MLA_HEADS, HEAD_PAD, t), lambda bi, i: (bi, 0, 0, 0)),
                  pl.BlockSpec((1, MLA_HEADS, t, HEAD_PAD), lambda bi, i: (bi, 0, 0, 0))],
        out_specs=pl.BlockSpec((1, tq, MLA_OUT), lambda bi, i: (bi, i, 0)),
        out_shape=jax.ShapeDtypeStruct((b, t, MLA_OUT), F32),
        compiler_params=_cp(2),
        name="mla_attention",
    )(q, kt, v)


def _rwkv_streams(u_ref, up_ref, mu_ref, w0_ref, wup_ref, a0_ref, aup_ref, gup_ref, kk_ref, ka_ref,
                  rk_ref, ones_ref):
    i = pl.program_id(1)
    u = u_ref[0]
    prow = jnp.where(i > 0, up_ref[0][7:8, :], 0.0)
    rowi = lax.broadcasted_iota(I32, u.shape, 0)
    prev = jnp.where(rowi == 0, prow, pltpu.roll(u, 1, axis=0))
    uf = u + (prev - u) * mu_ref[...]
    c = RWKV_CH
    r = uf[:, 0:c]
    k = uf[:, c:2 * c]
    v = uf[:, 2 * c:3 * c]
    wa = uf[:, 3 * c:3 * c + LANES]
    gd = uf[:, 3 * c + LANES:]
    ones_bd = ones_ref[...]
    wl = w0_ref[...] + _mm(jnp.tanh(wa), wup_ref[...])
    z = -wl
    softplus = jnp.maximum(z, 0.0) + jnp.log(1.0 + jnp.exp(-jnp.abs(z)))
    lw = -jnp.exp(-softplus - 0.5)
    a_s = _sigmoid(a0_ref[...] + _mm(wa, aup_ref[...]))
    g = _mm(_sigmoid(gd), gup_ref[...])
    kk = k * kk_ref[...]
    nrm = jnp.maximum(jnp.sqrt(_seg_sum(kk * kk, ones_bd)), 1e-12)
    kkn = kk / nrm
    k2 = k * (1.0 + (a_s - 1.0) * ka_ref[...])
    bonus = _seg_sum(r * k2 * rk_ref[...], ones_bd) * v
    return r, lw, k2, v, -kkn, kkn * a_s, bonus, g


def _intra_chunks(rs, lws, ks, vs, as_, bs, mask_bd, eye_bd, tril, strict, incl, eye_p):
    c = CHUNK
    each = lambda fn, *lists: [fn(*xs) for xs in zip(*lists)]

    def bd(x):
        xb = x.astype(BF16)
        return jnp.where(mask_bd, jnp.concatenate([xb, xb, xb, xb], axis=0), jnp.zeros((), BF16))

    def prefix(lw):
        hi = lw.astype(BF16)
        rem = lw - hi.astype(F32)
        mid = rem.astype(BF16)
        lo = (rem - mid.astype(F32)).astype(BF16)
        return (jnp.dot(tril, hi, preferred_element_type=F32) + jnp.dot(tril, mid, preferred_element_type=F32)
                + jnp.dot(tril, lo, preferred_element_type=F32))

    cums = each(prefix, lws)
    tots = each(lambda cum: cum[c - 1:c, :], cums)
    e_negs = each(lambda cum: jnp.exp(-cum), cums)
    e_ends = each(lambda tot, cum: jnp.exp(tot - cum), tots, cums)
    rts = each(lambda r, cum: r * jnp.exp(cum), rs, cums)
    ats = each(lambda a, cum, lw: a * jnp.exp(cum - lw), as_, cums, lws)
    stacks = each(lambda at, rt: jnp.concatenate([at, rt], axis=0), ats, rts)
    s1s = each(lambda st, b, e: _mm_nt(st, bd(b * e)), stacks, bs, e_negs)
    s2s = each(lambda st, k, e: _mm_nt(st, bd(k * e)), stacks, ks, e_negs)
    a_abs = each(lambda s1: jnp.where(strict, s1[:c], 0.0), s1s)
    m_rbs = each(lambda s1: jnp.where(incl, s1[c:], 0.0), s1s)
    a_aks = each(lambda s2: jnp.where(strict, s2[:c], 0.0), s2s)
    m_rks = each(lambda s2: jnp.where(incl, s2[c:], 0.0), s2s)
    xs = each(lambda a_ab: eye_p + a_ab, a_abs)
    aks = each(lambda a_ab: _mm(a_ab, bd(a_ab)), a_abs)
    for j in range(5):
        if j < 4:
            pps = each(lambda ak, x: _mm(jnp.concatenate([ak, x], axis=0), bd(ak)), aks, xs)
            xs = each(lambda x, pp: x + pp[c:], xs, pps)
            aks = each(lambda pp: pp[:c], pps)
        else:
            xs = each(lambda x, ak: x + _mm(x, bd(ak)), xs, aks)
    bdvs = each(bd, vs)
    avs = each(_mm, a_aks, bdvs)
    ahs = each(lambda x, at: _mm(x, bd(at)), xs, ats)
    uhs = each(lambda x, av: _mm(x, bd(av)), xs, avs)
    qs = each(lambda rt, m_rb, ah: rt + _mm(m_rb, bd(ah)), rts, m_rbs, ahs)
    y0s = each(lambda m_rb, uh, m_rk, bdv: _mm(m_rb, bd(uh)) + _mm(m_rk, bdv), m_rbs, uhs, m_rks, bdvs)
    gs = each(lambda b, e, ah, tot: jnp.where(mask_bd, _mm_tn(b * e, ah), 0.0) + jnp.where(eye_bd, jnp.exp(tot), 0.0),
              bs, e_ends, ahs, tots)
    fs = each(lambda b, k, e, uh, v: jnp.where(mask_bd, _mm_tn(b * e, uh) + _mm_tn(k * e, v), 0.0),
              bs, ks, e_ends, uhs, vs)
    return qs, y0s, gs, fs


def _rwkv_intra_kernel(*refs, cps):
    q_o, y0_o, g_o, f_o, bonus_o, gate_o = refs[-6:]
    r, lw, k, v, a, b, bonus, gate = _rwkv_streams(*refs[:-6])
    bonus_o[0] = bonus
    gate_o[0] = gate
    c = CHUNK
    n = RWKV_CH
    ri = lax.broadcasted_iota(I32, (n, n), 0)
    ci = lax.broadcasted_iota(I32, (n, n), 1)
    mask_bd = (ri // RWKV_HEAD) == (ci // RWKV_HEAD)
    eye_bd = ri == ci
    t64 = lax.broadcasted_iota(I32, (c, c), 0)
    i64 = lax.broadcasted_iota(I32, (c, c), 1)
    tril = jnp.where(i64 <= t64, 1.0, 0.0).astype(BF16)
    tp = lax.broadcasted_iota(I32, (c, n), 0)
    ip = lax.broadcasted_iota(I32, (c, n), 1) & (RWKV_HEAD - 1)
    strict = ip < tp
    incl = ip <= tp
    eye_p = jnp.where(ip == tp, 1.0, 0.0)
    sls = [slice(s * c, (s + 1) * c) for s in range(cps)]
    split = lambda val: [val[sl] for sl in sls]
    qs, y0s, gs, fs = _intra_chunks(split(r), split(lw), split(k), split(v), split(a), split(b),
                                    mask_bd, eye_bd, tril, strict, incl, eye_p)
    for s, sl in enumerate(sls):
        q_o[0, sl, :] = qs[s]
        y0_o[0, sl, :] = y0s[s]
        g_o[0, s] = gs[s]
        f_o[0, s] = fs[s]


def _rwkv_intra(u3, p):
    b, t, _ = u3.shape
    n = RWKV_CH
    nc = t // CHUNK
    cps = max(d for d in (1, 2, 4, 8) if nc % d == 0)
    tt = cps * CHUNK
    hb = tt // _SUBLANES
    full = lambda shape: pl.BlockSpec(shape, lambda bi, i: (0,) * len(shape))
    tok = pl.BlockSpec((1, tt, n), lambda bi, i: (bi, i, 0))
    mat = pl.BlockSpec((1, cps, n, n), lambda bi, i: (bi, i, 0, 0))
    tok_shape = jax.ShapeDtypeStruct((b, t, n), F32)
    mat_shape = jax.ShapeDtypeStruct((b, nc, n, n), F32)
    return pl.pallas_call(
        functools.partial(_rwkv_intra_kernel, cps=cps),
        grid=(b, nc // cps),
        in_specs=[pl.BlockSpec((1, tt, RWKV_IN), lambda bi, i: (bi, i, 0)),
                  pl.BlockSpec((1, _SUBLANES, RWKV_IN), lambda bi, i: (bi, jnp.maximum(i * hb - 1, 0), 0)),
                  full((1, RWKV_IN)), full((1, RWKV_CH)), full((LANES, RWKV_CH)), full((1, RWKV_CH)),
                  full((LANES, RWKV_CH)), full((LANES, RWKV_CH)), full((1, RWKV_CH)), full((1, RWKV_CH)),
                  full((1, RWKV_CH)), full((RWKV_CH, RWKV_CH))],
        out_specs=[tok, tok, mat, mat, tok, tok],
        out_shape=[tok_shape, tok_shape, mat_shape, mat_shape, tok_shape, tok_shape],
        compiler_params=_cp(2),
        name="rwkv_intra",
    )(u3, u3, p["mu"], p["w0"], p["wup"], p["a0"], p["aup"], p["gup"], p["kk"], p["ka"], p["rk"], p["ones"])


def _rwkv_seq_kernel(q_ref, y0_ref, g_ref, f_ref, o_ref, h_ref, *, nb):
    c = pl.program_id(0)

    @pl.when(c == 0)
    def _():
        h_ref[...] = jnp.zeros_like(h_ref)

    for b in range(nb):
        hb = h_ref[b].astype(BF16)
        o_ref[b] = jnp.dot(q_ref[b].astype(BF16), hb, preferred_element_type=F32) + y0_ref[b]
        h_ref[b] = jnp.dot(g_ref[b, 0].astype(BF16), hb, preferred_element_type=F32) + f_ref[b, 0]


def _rwkv_seq(q, y0, g, f):
    b, t, n = q.shape
    nc = t // CHUNK
    tok = pl.BlockSpec((b, CHUNK, n), lambda c: (0, c, 0))
    mat = pl.BlockSpec((b, 1, n, n), lambda c: (0, c, 0, 0))
    return pl.pallas_call(
        functools.partial(_rwkv_seq_kernel, nb=b),
        grid=(nc,),
        in_specs=[tok, tok, mat, mat],
        out_specs=tok,
        out_shape=jax.ShapeDtypeStruct((b, t, n), F32),
        scratch_shapes=[pltpu.VMEM((b, n, n), F32)],
        compiler_params=_cp(1),
        name="rwkv_seq",
    )(q, y0, g, f)


def _store_token_tiles(ref, val, tm):
    for p in range(D_MODEL // LANES):
        ref[pl.ds(p, tm, stride=_ROW_TILE), :] = val[:, p * LANES:(p + 1) * LANES]


def _load_token_tiles(ref, tm):
    return jnp.concatenate([ref[pl.ds(p, tm, stride=_ROW_TILE), :] for p in range(D_MODEL // LANES)], axis=-1)


def _out_proj_kernel(x_ref, yc_ref, ym_ref, yw_ref, bonus_ref, gate_ref, lng_ref, lnb_ref, ones_ref,
                     w_ref, gt_ref, g2_ref, sc_ref, sh_ref, *rest, with_router, tm):
    if with_router:
        wr_ref, xo_ref, h_ref, lg_ref = rest
    else:
        xo_ref, h_ref = rest
    ones_bd = ones_ref[...]
    inv = 1.0 / RWKV_HEAD
    y = yw_ref[...]
    mu = _seg_sum(y, ones_bd) * inv
    yc = y - mu
    var = _seg_sum(yc * yc, ones_bd) * inv
    yn = yc * lax.rsqrt(var + RWKV_GN_EPS) * lng_ref[...] + lnb_ref[...]
    yr = (yn + bonus_ref[...]) * gate_ref[...]
    mixed = (_mm(yc_ref[...], w_ref[0:CONV_CH, :])
             + _mm(ym_ref[...], w_ref[CONV_CH:CONV_CH + MLA_OUT, :])
             + _mm(yr, w_ref[CONV_CH + MLA_OUT:, :]))
    x = x_ref[...] + gt_ref[0] * mixed
    xo_ref[...] = x
    ms = jnp.mean(x * x, axis=-1, keepdims=True)
    h = x * lax.rsqrt(ms + RMS_EPS) * g2_ref[...]
    h = h * (1.0 + sc_ref[0]) + sh_ref[0]
    if with_router:
        _store_token_tiles(h_ref, h, tm)
    else:
        h_ref[...] = h
    if with_router:
        h_hi, h_lo = _split2(h)
        w_hi, w_lo = _split2(wr_ref[...])
        lg_ref[...] = (jnp.dot(h_hi, w_hi, preferred_element_type=F32)
                       + jnp.dot(h_lo, w_hi, preferred_element_type=F32)
                       + jnp.dot(h_hi, w_lo, preferred_element_type=F32))


def _out_proj(x2, yc, ym, yw, bonus, gate, ln_g, ln_b, ones_bd, w_out, gt, g2, sc, sh, router, t):
    n = x2.shape[0]
    tm = min(512, t)
    tpb = t // tm
    vec = pl.BlockSpec((1, 1, D_MODEL), lambda i: (i // tpb, 0, 0))
    row = lambda w: pl.BlockSpec((tm, w), lambda i: (i, 0))
    full = lambda r, c: pl.BlockSpec((r, c), lambda i: (0, 0))
    in_specs = [row(D_MODEL), row(CONV_CH), row(MLA_OUT), row(RWKV_CH), row(RWKV_CH), row(RWKV_CH),
                full(1, RWKV_CH), full(1, RWKV_CH), full(RWKV_CH, RWKV_CH),
                full(D_MODEL, D_MODEL), vec, full(1, D_MODEL), vec, vec]
    out_specs = [row(D_MODEL), row(D_MODEL)]
    out_shape = [jax.ShapeDtypeStruct((n, D_MODEL), F32), jax.ShapeDtypeStruct((n, D_MODEL), F32)]
    args = [x2, yc, ym, yw, bonus, gate, ln_g, ln_b, ones_bd, w_out, gt, g2, sc, sh]
    if router is not None:
        in_specs.append(full(D_MODEL, LANES))
        out_specs = [row(D_MODEL), pl.BlockSpec((tm * _ROW_TILE, LANES), lambda i: (i, 0)), row(LANES)]
        out_shape = [out_shape[0], jax.ShapeDtypeStruct((n * _ROW_TILE, LANES), F32),
                     jax.ShapeDtypeStruct((n, LANES), F32)]
        args.append(router)
    return pl.pallas_call(
        functools.partial(_out_proj_kernel, with_router=router is not None, tm=tm),
        grid=(n // tm,),
        in_specs=in_specs, out_specs=out_specs, out_shape=out_shape,
        compiler_params=_cp(1),
        name="out_proj",
    )(*args)


def _swiglu_step(load_h, wg_ref, wu_ref, wd_ref, hb_ref, acc_ref, j):
    @pl.when(j == 0)
    def _():
        hb_ref[...] = load_h().astype(BF16)
        acc_ref[...] = jnp.zeros_like(acc_ref)

    hb = hb_ref[...]
    gate = jnp.dot(hb, wg_ref[...], preferred_element_type=F32)
    up = jnp.dot(hb, wu_ref[...], preferred_element_type=F32)
    act = (gate * _sigmoid(gate) * up).astype(BF16)
    acc_ref[...] += jnp.dot(act, wd_ref[...], preferred_element_type=F32)


def _ffn_kernel(h_ref, x_ref, gt_ref, wg_ref, wu_ref, wd_ref, o_ref, hb_ref, acc_ref, *, nj):
    j = pl.program_id(1)
    _swiglu_step(lambda: h_ref[...], wg_ref, wu_ref, wd_ref, hb_ref, acc_ref, j)

    @pl.when(j == nj - 1)
    def _():
        o_ref[...] = x_ref[...] + gt_ref[0] * acc_ref[...]


def _dense_ffn(h2, x2, gt, wg, wu, wd, t):
    n = x2.shape[0]
    ff = wg.shape[1]
    tm = min(512, t)
    tpb = t // tm
    nj = 2
    tf = ff // nj
    row = pl.BlockSpec((tm, D_MODEL), lambda i, j: (i, 0))
    return pl.pallas_call(
        functools.partial(_ffn_kernel, nj=nj),
        grid=(n // tm, nj),
        in_specs=[row, row, pl.BlockSpec((1, 1, D_MODEL), lambda i, j: (i // tpb, 0, 0)),
                  pl.BlockSpec((D_MODEL, tf), lambda i, j: (0, j)),
                  pl.BlockSpec((D_MODEL, tf), lambda i, j: (0, j)),
                  pl.BlockSpec((tf, D_MODEL), lambda i, j: (j, 0))],
        out_specs=row,
        out_shape=jax.ShapeDtypeStruct((n, D_MODEL), F32),
        scratch_shapes=[pltpu.VMEM((tm, D_MODEL), BF16), pltpu.VMEM((tm, D_MODEL), F32)],
        compiler_params=_cp(2),
        name="dense_ffn",
    )(h2, x2, gt, wg, wu, wd)


def _moe_ffn_kernel(te_ref, nu_ref, xs_ref, wg_ref, wu_ref, wd_ref, o_ref, hb_ref, acc_ref, *, nj, tm):
    i = pl.program_id(0)
    j = pl.program_id(1)
    used = i < nu_ref[0]

    @pl.when(used)
    def _():
        _swiglu_step(lambda: _load_token_tiles(xs_ref, tm), wg_ref.at[0], wu_ref.at[0], wd_ref.at[0],
                     hb_ref, acc_ref, j)

    @pl.when(j == nj - 1)
    def _():
        _store_token_tiles(o_ref, jnp.where(used, acc_ref[...], 0.0), tm)


def _moe_ffn(xs, tile_e, n_used, wg, wu, wd):
    cap = xs.shape[0] // _ROW_TILE
    ff = wg.shape[2]
    nj = 2
    tf = ff // nj
    tm = MOE_TILE

    def wj(i, j, nu):
        return jnp.where(i < nu[0], j, nj - 1)

    row = pl.BlockSpec((tm * _ROW_TILE, LANES), lambda i, j, te, nu: (i, 0))
    grid_spec = pltpu.PrefetchScalarGridSpec(
        num_scalar_prefetch=2,
        grid=(cap // tm, nj),
        in_specs=[row,
                  pl.BlockSpec((1, D_MODEL, tf), lambda i, j, te, nu: (te[i], 0, wj(i, j, nu))),
                  pl.BlockSpec((1, D_MODEL, tf), lambda i, j, te, nu: (te[i], 0, wj(i, j, nu))),
                  pl.BlockSpec((1, tf, D_MODEL), lambda i, j, te, nu: (te[i], wj(i, j, nu), 0))],
        out_specs=row,
        scratch_shapes=[pltpu.VMEM((tm, D_MODEL), BF16), pltpu.VMEM((tm, D_MODEL), F32)])
    return pl.pallas_call(
        functools.partial(_moe_ffn_kernel, nj=nj, tm=tm),
        grid_spec=grid_spec,
        out_shape=jax.ShapeDtypeStruct((cap * _ROW_TILE, LANES), F32),
        compiler_params=_cp(2),
        name="moe_ffn",
    )(tile_e, n_used, xs, wg, wu, wd)


def _route_kernel(lg_ref, oi_ref, of_ref, cnt_ref, carry_ref, *, tm):
    i = pl.program_id(0)

    @pl.when(i == 0)
    def _():
        carry_ref[...] = jnp.zeros_like(carry_ref)

    lane = lax.broadcasted_iota(I32, (tm, LANES), 1)
    real = lane < N_EXPERTS
    lg = jnp.where(real, lg_ref[...], _NEG)
    ex = jnp.exp(lg - jnp.max(lg, axis=-1, keepdims=True))
    p = ex / jnp.sum(ex, axis=-1, keepdims=True)
    p = jnp.where(real, p, -1.0)
    m1 = jnp.max(p, axis=-1, keepdims=True)
    i1 = jnp.min(jnp.where(p == m1, lane, LANES), axis=-1, keepdims=True)
    p2 = jnp.where(lane == i1, -1.0, p)
    m2 = jnp.max(p2, axis=-1, keepdims=True)
    i2 = jnp.min(jnp.where(p2 == m2, lane, LANES), axis=-1, keepdims=True)
    den = m1 + m2
    oh1 = lane == i1
    oh2 = lane == i2
    oh = jnp.where(oh1 | oh2, 1.0, 0.0)
    rr = lax.broadcasted_iota(I32, (tm, tm), 0)
    cc = lax.broadcasted_iota(I32, (tm, tm), 1)
    lstrict = jnp.where(cc < rr, 1.0, 0.0).astype(BF16)
    pre = jnp.dot(lstrict, oh.astype(BF16), preferred_element_type=F32) + carry_ref[...]
    rank1 = jnp.sum(jnp.where(oh1, pre, 0.0), axis=-1, keepdims=True).astype(I32)
    rank2 = jnp.sum(jnp.where(oh2, pre, 0.0), axis=-1, keepdims=True).astype(I32)
    carry_ref[...] = carry_ref[...] + jnp.sum(oh, axis=0, keepdims=True)
    cnt_ref[...] = carry_ref[...]
    oi_ref[...] = jnp.where(lane == 0, i1, jnp.where(lane == 1, i2, jnp.where(lane == 2, rank1,
                                                                                jnp.where(lane == 3, rank2, 0))))
    of_ref[...] = jnp.where(lane == 0, m1 / den, jnp.where(lane == 1, m2 / den, 0.0))


def _route(logits, t):
    n = logits.shape[0]
    tm = min(512, t)
    row = pl.BlockSpec((tm, LANES), lambda i: (i, 0))
    return pl.pallas_call(
        functools.partial(_route_kernel, tm=tm),
        grid=(n // tm,),
        in_specs=[row],
        out_specs=[row, row, pl.BlockSpec((1, LANES), lambda i: (0, 0))],
        out_shape=[jax.ShapeDtypeStruct((n, LANES), I32), jax.ShapeDtypeStruct((n, LANES), F32),
                   jax.ShapeDtypeStruct((1, LANES), F32)],
        scratch_shapes=[pltpu.VMEM((1, LANES), F32)],
        compiler_params=_cp(1),
        name="moe_route",
    )(logits)


def _dispatch_kernel(s0_ref, s1_ref, h_ref, init_ref, xs_ref, sem, *, tm):
    del init_ref
    base = pl.program_id(0) * tm

    def row_copy(r, slot):
        return pltpu.make_async_copy(h_ref.at[pl.ds(pl.multiple_of(r * _ROW_TILE, _ROW_TILE), _ROW_TILE)],
                                     xs_ref.at[pl.ds(pl.multiple_of(slot * _ROW_TILE, _ROW_TILE), _ROW_TILE)],
                                     sem)

    def start(r, carry):
        row_copy(r, s0_ref[base + r]).start(priority=0)
        row_copy(r, s1_ref[base + r]).start(priority=1)
        return carry

    lax.fori_loop(0, tm, start, 0)

    def wait(r, carry):
        row_copy(r, s0_ref[base + r]).wait()
        row_copy(r, s1_ref[base + r]).wait()
        return carry

    lax.fori_loop(0, tm, wait, 0)


def _dispatch(h2, slot0, slot1, cap, t):
    n = h2.shape[0] // _ROW_TILE
    tm = min(512, t)
    grid_spec = pltpu.PrefetchScalarGridSpec(
        num_scalar_prefetch=2,
        grid=(n // tm,),
        in_specs=[pl.BlockSpec((tm * _ROW_TILE, LANES), lambda i, s0, s1: (i, 0)),
                  pl.BlockSpec(memory_space=pl.ANY)],
        out_specs=pl.BlockSpec(memory_space=pl.ANY),
        scratch_shapes=[pltpu.SemaphoreType.DMA(())])
    return pl.pallas_call(
        functools.partial(_dispatch_kernel, tm=tm),
        grid_spec=grid_spec,
        out_shape=jax.ShapeDtypeStruct((cap * _ROW_TILE, LANES), F32),
        input_output_aliases={3: 0},
        compiler_params=_cp(1),
        name="moe_dispatch",
    )(slot0, slot1, h2, jnp.zeros((cap * _ROW_TILE, LANES), F32))


def _combine_kernel(s0_ref, s1_ref, x_ref, wf_ref, gt_ref, ys_ref, o_ref, buf0, buf1, sem, *, tm):
    base = pl.program_id(0) * tm

    def row_copy(slot, buf, r):
        return pltpu.make_async_copy(ys_ref.at[pl.ds(pl.multiple_of(slot * _ROW_TILE, _ROW_TILE), _ROW_TILE)],
                                     buf.at[pl.ds(pl.multiple_of(r * _ROW_TILE, _ROW_TILE), _ROW_TILE)], sem)

    def start(r, carry):
        row_copy(s0_ref[base + r], buf0, r).start(priority=0)
        row_copy(s1_ref[base + r], buf1, r).start(priority=1)
        return carry

    lax.fori_loop(0, tm, start, 0)

    def wait(r, carry):
        row_copy(s0_ref[base + r], buf0, r).wait()
        row_copy(s1_ref[base + r], buf1, r).wait()
        return carry

    lax.fori_loop(0, tm, wait, 0)
    wf = wf_ref[...]
    f = wf[:, 0:1] * _load_token_tiles(buf0, tm) + wf[:, 1:2] * _load_token_tiles(buf1, tm)
    o_ref[...] = x_ref[...] + gt_ref[0] * f


def _combine(x2, ys, slot0, slot1, wf, gt, t):
    n = x2.shape[0]
    tm = min(512, t)
    tpb = t // tm
    row = pl.BlockSpec((tm, D_MODEL), lambda i, s0, s1: (i, 0))
    grid_spec = pltpu.PrefetchScalarGridSpec(
        num_scalar_prefetch=2,
        grid=(n // tm,),
        in_specs=[row, pl.BlockSpec((tm, LANES), lambda i, s0, s1: (i, 0)),
                  pl.BlockSpec((1, 1, D_MODEL), lambda i, s0, s1: (i // tpb, 0, 0)),
                  pl.BlockSpec(memory_space=pl.ANY)],
        out_specs=row,
        scratch_shapes=[pltpu.VMEM((tm * _ROW_TILE, LANES), F32), pltpu.VMEM((tm * _ROW_TILE, LANES), F32),
                        pltpu.SemaphoreType.DMA(())])
    return pl.pallas_call(
        functools.partial(_combine_kernel, tm=tm),
        grid_spec=grid_spec,
        out_shape=jax.ShapeDtypeStruct((n, D_MODEL), F32),
        compiler_params=_cp(1),
        name="moe_combine",
    )(slot0, slot1, x2, wf, gt, ys)


def _moe(h2, x2, logits, gt, wg, wu, wd, t):
    n = x2.shape[0]
    info_i, info_f, cnt = _route(logits, t)
    counts = cnt[0, :N_EXPERTS].astype(I32)
    padded = (counts + MOE_TILE - 1) // MOE_TILE * MOE_TILE
    pad_ends = jnp.cumsum(padded)
    pad_starts = pad_ends - padded
    slot0 = pad_starts[info_i[:, 0]] + info_i[:, 2]
    slot1 = pad_starts[info_i[:, 1]] + info_i[:, 3]
    cap = (2 * n // MOE_TILE + N_EXPERTS) * MOE_TILE
    n_tiles = cap // MOE_TILE
    n_used = (pad_ends[-1:] // MOE_TILE).astype(I32)
    tile_e = jnp.minimum(jnp.searchsorted(pad_ends, jnp.arange(n_tiles, dtype=I32) * MOE_TILE, side="right"),
                         N_EXPERTS - 1).astype(I32)
    last_e = tile_e[jnp.maximum(n_used[0] - 1, 0)]
    tile_e = jnp.where(jnp.arange(n_tiles) < n_used[0], tile_e, last_e)
    xs = _dispatch(h2, slot0, slot1, cap, t)
    ys = _moe_ffn(xs, tile_e, n_used, wg, wu, wd)
    return _combine(x2, ys, slot0, slot1, info_f, gt, t)


def _cast_kernel(a_ref, b_ref, c_ref, ao_ref, bo_ref, co_ref):
    ao_ref[...] = a_ref[...].astype(BF16)
    bo_ref[...] = b_ref[...].astype(BF16)
    co_ref[...] = c_ref[...].astype(BF16)


def _expert_weights_bf16(layer, w_gate, w_up, w_down):
    _, ne, d, ff = w_gate.shape
    rows = 256
    gu = pl.BlockSpec((None, 1, rows, ff), lambda e, r: (layer, e, r, 0))
    dn = pl.BlockSpec((None, 1, rows * ff // d, d), lambda e, r: (layer, e, r, 0))
    gu_o = pl.BlockSpec((1, rows, ff), lambda e, r: (e, r, 0))
    dn_o = pl.BlockSpec((1, rows * ff // d, d), lambda e, r: (e, r, 0))
    return pl.pallas_call(
        _cast_kernel,
        grid=(ne, d // rows),
        in_specs=[gu, gu, dn],
        out_specs=[gu_o, gu_o, dn_o],
        out_shape=[jax.ShapeDtypeStruct((ne, d, ff), BF16), jax.ShapeDtypeStruct((ne, d, ff), BF16),
                   jax.ShapeDtypeStruct((ne, ff, d), BF16)],
        compiler_params=_cp(2),
        name="expert_weights_bf16",
    )(w_gate, w_up, w_down)


def _pad_last(a, width):
    return jnp.pad(a, [(0, 0)] * (a.ndim - 1) + [(0, width - a.shape[-1])])


def _swap_rope(a):
    h = MLA_ROPE // 2
    return jnp.concatenate([a[..., :MLA_NOPE], a[..., MLA_NOPE + h:], a[..., MLA_NOPE:MLA_NOPE + h]], axis=-1)


def _layer_params(l, w_in, mla_q_norm_g, mla_w_uq, mla_kv_norm_g, mla_w_ukv, qk_norm_q, qk_norm_k,
                  rwkv_mu, rwkv_w0, rwkv_w_up, rwkv_a0, rwkv_a_up, rwkv_g_up, rwkv_k_k, rwkv_k_a, rwkv_r_k):
    w = w_in[l]
    m0 = CONV_IN
    kr0 = m0 + MLA_Q_LORA + MLA_KV_LORA
    kr = w[:, kr0:kr0 + MLA_ROPE]
    h = MLA_ROPE // 2
    zeros = lambda c: jnp.zeros((D_MODEL, c), F32)
    kr_tile = jnp.concatenate([zeros(MLA_NOPE), kr, zeros(LANES - MLA_QK)], axis=1)
    krs_tile = jnp.concatenate([zeros(MLA_NOPE), kr[:, h:], kr[:, :h], zeros(LANES - MLA_QK)], axis=1)
    w_in_p = jnp.concatenate([w[:, :kr0], kr_tile, krs_tile, w[:, kr0 + MLA_ROPE:]], axis=1).astype(BF16)

    wq = mla_w_uq[l].reshape(MLA_Q_LORA, MLA_HEADS, MLA_QK)
    wkv = mla_w_ukv[l].reshape(MLA_KV_LORA, MLA_HEADS, MLA_NOPE + MLA_V)
    flat = lambda a: _pad_last(a, HEAD_PAD).reshape(a.shape[0], MLA_HEADS * HEAD_PAD).astype(BF16)
    mla = dict(
        gql=mla_q_norm_g[l].reshape(1, -1), gkvl=mla_kv_norm_g[l].reshape(1, -1),
        wq=flat(wq), wqs=flat(_swap_rope(wq)),
        wk=flat(wkv[..., :MLA_NOPE]), wv=flat(wkv[..., MLA_NOPE:]),
        gq=_pad_last(qk_norm_q[l], LANES).reshape(1, LANES),
        gqs=_pad_last(_swap_rope(qk_norm_q[l]), LANES).reshape(1, LANES),
        gk=_pad_last(qk_norm_k[l], LANES).reshape(1, LANES),
        gks=_pad_last(_swap_rope(qk_norm_k[l]), LANES).reshape(1, LANES))

    zl = jnp.zeros((LANES // 2, RWKV_CH), F32)
    hid = jnp.arange(RWKV_CH) // RWKV_HEAD
    rwkv = dict(
        mu=rwkv_mu[l].reshape(1, -1), w0=rwkv_w0[l].reshape(1, -1), a0=rwkv_a0[l].reshape(1, -1),
        wup=jnp.concatenate([rwkv_w_up[l], zl], axis=0).astype(BF16),
        aup=jnp.concatenate([zl, rwkv_a_up[l]], axis=0).astype(BF16),
        gup=rwkv_g_up[l].astype(BF16),
        kk=rwkv_k_k[l].reshape(1, -1), ka=rwkv_k_a[l].reshape(1, -1), rk=rwkv_r_k[l].reshape(1, -1),
        ones=(hid[:, None] == hid[None, :]).astype(BF16))
    return w_in_p, mla, rwkv


def kernel(x, c, positions, ada_w, ada_b, ada_layer_bias, norm1_g, norm2_g, w_in, w_out, conv_w, conv_b, conv_ln_g, conv_ln_b, mla_q_norm_g, mla_w_uq, mla_kv_norm_g, mla_w_ukv, qk_norm_q, qk_norm_k, rwkv_mu, rwkv_w0, rwkv_w_up, rwkv_a0, rwkv_a_up, rwkv_g_up, rwkv_k_k, rwkv_k_a, rwkv_r_k, rwkv_ln_g, rwkv_ln_b, ffn_w_gate, ffn_w_up, ffn_w_down, moe_router, moe_w_gate, moe_w_up, moe_w_down):
    b, t, d = x.shape
    n = b * t
    mods = _cond(c, ada_w, ada_b, ada_layer_bias)
    cos_t, sin_t = _rope_tables(positions)
    x2 = x.reshape(n, d)
    for l in range(DEPTH):
        sh1, sc1, gt1, sh2, sc2, gt2 = [mods[l, :, k * d:(k + 1) * d].reshape(b, 1, d) for k in range(N_MOD)]
        w_in_p, mla_p, rwkv_p = _layer_params(
            l, w_in, mla_q_norm_g, mla_w_uq, mla_kv_norm_g, mla_w_ukv, qk_norm_q, qk_norm_k,
            rwkv_mu, rwkv_w0, rwkv_w_up, rwkv_a0, rwkv_a_up, rwkv_g_up, rwkv_k_k, rwkv_k_a, rwkv_r_k)
        u_conv, u_mla, u_rwkv = _in_proj(x2, sc1, sh1, norm1_g[l].reshape(1, d), w_in_p, t)
        y_conv = _conformer_conv(u_conv.reshape(b, t, CONV_IN), conv_w[l], conv_b[l], conv_ln_g[l], conv_ln_b[l])
        q, kt, v = _mla_prep(u_mla.reshape(b, t, MLA_PAD_IN), cos_t, sin_t, mla_p)
        y_mla = _attention(q, kt, v)
        qc, y0, g_, f_, bonus, gate = _rwkv_intra(u_rwkv.reshape(b, t, RWKV_IN), rwkv_p)
        y_wkv = _rwkv_seq(qc, y0, g_, f_)
        i = l // 2
        router = None
        if l % 2 == 1:
            router = _pad_last(moe_router[i], LANES)
        flat = lambda a: a.reshape(n, a.shape[-1])
        outs = _out_proj(x2, flat(y_conv), flat(y_mla), flat(y_wkv), flat(bonus), flat(gate),
                         rwkv_ln_g[l].reshape(1, -1), rwkv_ln_b[l].reshape(1, -1), rwkv_p["ones"],
                         w_out[l].astype(BF16), gt1, norm2_g[l].reshape(1, d), sc2, sh2, router, t)
        if l % 2 == 0:
            x2, h2 = outs
            x2 = _dense_ffn(h2, x2, gt2, ffn_w_gate[i].astype(BF16), ffn_w_up[i].astype(BF16),
                            ffn_w_down[i].astype(BF16), t)
        else:
            x2, h2, logits = outs
            wg_b, wu_b, wd_b = _expert_weights_bf16(i, moe_w_gate, moe_w_up, moe_w_down)
            x2 = _moe(h2, x2, logits, gt2, wg_b, wu_b, wd_b, t)
    return x2.reshape(b, t, d)
```

```python
import functools

import jax
import jax.numpy as jnp
from jax import lax
from jax.experimental import pallas as pl
from jax.experimental.pallas import tpu as pltpu

F32 = jnp.float32
BF16 = jnp.bfloat16
I32 = jnp.int32

D_MODEL = 1024
DEPTH = 4
N_MOD = 6
CONV_CH = 256
CONV_K = 31
CONV_IN = 2 * CONV_CH
MLA_NOPE = 64
MLA_ROPE = 32
MLA_V = 64
MLA_QK = MLA_NOPE + MLA_ROPE
MLA_HEADS = 8
MLA_OUT = MLA_HEADS * MLA_V
MLA_Q_LORA = 256
MLA_KV_LORA = 128
MLA_IN = MLA_Q_LORA + MLA_KV_LORA + MLA_ROPE
ROPE_THETA = 10000.0
RWKV_HEAD = 64
RWKV_CH = 256
RWKV_HEADS = 4
RWKV_IN = 1024
N_EXPERTS = 8
RMS_EPS = 1e-6
LN_EPS = 1e-5
RWKV_GN_EPS = 64e-5

LANES = 128
HEAD_PAD = LANES
MLA_PAD_IN = MLA_Q_LORA + MLA_KV_LORA + 2 * LANES
IN_PAD = CONV_IN + MLA_PAD_IN + RWKV_IN
CHUNK = 64
MOE_TILE = 512
_ROW_TILE = D_MODEL // LANES
VMEM_LIMIT = 56 * 1024 * 1024

_NEG = -1e30


def _cp(n_axes, vmem=None):
    return pltpu.CompilerParams(dimension_semantics=("arbitrary",) * n_axes,
                                vmem_limit_bytes=vmem if vmem else VMEM_LIMIT)


def _mm(a, b):
    return jnp.dot(a.astype(BF16), b.astype(BF16), preferred_element_type=F32)


def _mm_nt(a, b):
    return lax.dot_general(a.astype(BF16), b.astype(BF16), (((1,), (1,)), ((), ())),
                           preferred_element_type=F32)


def _mm_tn(a, b):
    return lax.dot_general(a.astype(BF16), b.astype(BF16), (((0,), (0,)), ((), ())),
                           preferred_element_type=F32)


def _sigmoid(x):
    return 1.0 / (1.0 + jnp.exp(-x))


def _split2(x):
    hi = x.astype(BF16)
    lo = (x - hi.astype(F32)).astype(BF16)
    return hi, lo


def _seg_sum(x, ones_bd):
    hi, lo = _split2(x)
    return (jnp.dot(hi, ones_bd, preferred_element_type=F32)
            + jnp.dot(lo, ones_bd, preferred_element_type=F32))


def _cond_kernel(c_ref, w_ref, b_ref, lb_ref, o_ref):
    c = c_ref[...]
    s = c * _sigmoid(c)
    base = jnp.dot(s, w_ref[...], precision=lax.Precision.HIGHEST, preferred_element_type=F32) + b_ref[...]
    for l in range(DEPTH):
        o_ref[l] = base + lb_ref[l]


def _cond(c, ada_w, ada_b, ada_layer_bias):
    b = c.shape[0]
    n = ada_w.shape[1]
    tn = 1536
    cp = jnp.zeros((8, D_MODEL), F32).at[:b].set(c)
    out = pl.pallas_call(
        _cond_kernel,
        grid=(n // tn,),
        in_specs=[pl.BlockSpec((8, D_MODEL), lambda j: (0, 0)),
                  pl.BlockSpec((D_MODEL, tn), lambda j: (0, j)),
                  pl.BlockSpec((1, tn), lambda j: (0, j)),
                  pl.BlockSpec((DEPTH, 1, tn), lambda j: (0, 0, j))],
        out_specs=pl.BlockSpec((DEPTH, 8, tn), lambda j: (0, 0, j)),
        out_shape=jax.ShapeDtypeStruct((DEPTH, 8, n), F32),
        compiler_params=_cp(1),
        name="cond",
    )(cp, ada_w, ada_b.reshape(1, n), ada_layer_bias.reshape(DEPTH, 1, n))
    return out[:, :b]


def _rope_kernel(pos_ref, invf_ref, c_ref, s_ref):
    pos = pos_ref[0].astype(F32)
    ang = pos * invf_ref[...]
    lane = lax.broadcasted_iota(I32, ang.shape, 1)
    cs = jnp.cos(ang)
    sn = jnp.sin(ang)
    c_ref[0] = jnp.where(lane < MLA_NOPE, 1.0, jnp.where(lane < MLA_QK, cs, 0.0))
    s_ref[0] = jnp.where((lane >= MLA_NOPE) & (lane < MLA_NOPE + 16), -sn,
                         jnp.where((lane >= MLA_NOPE + 16) & (lane < MLA_QK), sn, 0.0))


def _rope_tables(positions):
    b, t = positions.shape
    tt = min(512, t)
    inv_freq = ROPE_THETA ** (-jnp.arange(0, MLA_ROPE, 2, dtype=F32) / MLA_ROPE)
    invf = jnp.zeros((1, LANES), F32)
    invf = invf.at[0, MLA_NOPE:MLA_NOPE + 16].set(inv_freq).at[0, MLA_NOPE + 16:MLA_QK].set(inv_freq)
    spec = pl.BlockSpec((1, tt, LANES), lambda bi, i: (bi, i, 0))
    return pl.pallas_call(
        _rope_kernel,
        grid=(b, t // tt),
        in_specs=[pl.BlockSpec((1, tt, 1), lambda bi, i: (bi, i, 0)),
                  pl.BlockSpec((1, LANES), lambda bi, i: (0, 0))],
        out_specs=[spec, spec],
        out_shape=[jax.ShapeDtypeStruct((b, t, LANES), F32)] * 2,
        compiler_params=_cp(2),
        name="rope_tables",
    )(positions.reshape(b, t, 1), invf)


def _in_proj_kernel(x_ref, sc_ref, sh_ref, g_ref, w_ref, oc_ref, om_ref, or_ref):
    x = x_ref[...]
    ms = jnp.mean(x * x, axis=-1, keepdims=True)
    h = x * lax.rsqrt(ms + RMS_EPS) * g_ref[...]
    h = h * (1.0 + sc_ref[0]) + sh_ref[0]
    hb = h.astype(BF16)
    oc_ref[...] = jnp.dot(hb, w_ref[:, 0:CONV_IN], preferred_element_type=F32)
    om_ref[...] = jnp.dot(hb, w_ref[:, CONV_IN:CONV_IN + MLA_PAD_IN], preferred_element_type=F32)
    or_ref[...] = jnp.dot(hb, w_ref[:, CONV_IN + MLA_PAD_IN:IN_PAD], preferred_element_type=F32)


def _in_proj(x2, sc, sh, g, w, t):
    n = x2.shape[0]
    tm = min(512, t)
    tpb = t // tm
    vec = pl.BlockSpec((1, 1, D_MODEL), lambda i: (i // tpb, 0, 0))
    return pl.pallas_call(
        _in_proj_kernel,
        grid=(n // tm,),
        in_specs=[pl.BlockSpec((tm, D_MODEL), lambda i: (i, 0)), vec, vec,
                  pl.BlockSpec((1, D_MODEL), lambda i: (0, 0)),
                  pl.BlockSpec((D_MODEL, IN_PAD), lambda i: (0, 0))],
        out_specs=[pl.BlockSpec((tm, CONV_IN), lambda i: (i, 0)),
                   pl.BlockSpec((tm, MLA_PAD_IN), lambda i: (i, 0)),
                   pl.BlockSpec((tm, RWKV_IN), lambda i: (i, 0))],
        out_shape=[jax.ShapeDtypeStruct((n, CONV_IN), F32),
                   jax.ShapeDtypeStruct((n, MLA_PAD_IN), F32),
                   jax.ShapeDtypeStruct((n, RWKV_IN), F32)],
        compiler_params=_cp(1),
        name="in_proj",
    )(x2, sc, sh, g, w)


_CONV_HALO = 32
_SUBLANES = 8


def _conv_kernel(u_ref, up_ref, w_ref, b_ref, g_ref, bb_ref, o_ref, zbuf, *, tt):
    i = pl.program_id(1)
    u = u_ref[0]
    z = u[:, :CONV_CH] * _sigmoid(u[:, CONV_CH:])
    up = up_ref[0]
    zp = up[:, :CONV_CH] * _sigmoid(up[:, CONV_CH:])
    zp = jnp.where(i > 0, zp, 0.0)
    zbuf[0:_CONV_HALO, :] = zp
    zbuf[_CONV_HALO:, :] = z
    acc = jnp.zeros((tt, CONV_CH), F32)
    first = _CONV_HALO - (CONV_K - 1)
    zfull = zbuf[...]
    nrows = tt + _CONV_HALO
    for sh in range(_SUBLANES):
        taps = [k for k in range(CONV_K) if (k + first) % _SUBLANES == sh]
        shifted = zfull if sh == 0 else pltpu.roll(zfull, nrows - sh, axis=0)
        for k in taps:
            a0 = k + first - sh
            acc = acc + shifted[a0:a0 + tt] * w_ref[k:k + 1, :]
    y = acc + b_ref[...]
    mu = jnp.mean(y, axis=-1, keepdims=True)
    yc = y - mu
    var = jnp.mean(yc * yc, axis=-1, keepdims=True)
    yn = yc * lax.rsqrt(var + LN_EPS) * g_ref[...] + bb_ref[...]
    o_ref[0] = yn * _sigmoid(yn)


def _conformer_conv(u3, conv_w, conv_b, ln_g, ln_b):
    b, t, _ = u3.shape
    tt = min(512, t)
    hb = tt // _CONV_HALO
    vec = pl.BlockSpec((1, CONV_CH), lambda bi, i: (0, 0))
    return pl.pallas_call(
        functools.partial(_conv_kernel, tt=tt),
        grid=(b, t // tt),
        in_specs=[pl.BlockSpec((1, tt, CONV_IN), lambda bi, i: (bi, i, 0)),
                  pl.BlockSpec((1, _CONV_HALO, CONV_IN), lambda bi, i: (bi, jnp.maximum(i * hb - 1, 0), 0)),
                  pl.BlockSpec((CONV_K, CONV_CH), lambda bi, i: (0, 0)), vec, vec, vec],
        out_specs=pl.BlockSpec((1, tt, CONV_CH), lambda bi, i: (bi, i, 0)),
        out_shape=jax.ShapeDtypeStruct((b, t, CONV_CH), F32),
        scratch_shapes=[pltpu.VMEM((tt + _CONV_HALO, CONV_CH), F32)],
        compiler_params=_cp(2),
        name="conformer_conv",
    )(u3, u3, conv_w, conv_b.reshape(1, -1), ln_g.reshape(1, -1), ln_b.reshape(1, -1))


def _mla_prep_kernel(um_ref, c_ref, s_ref, gql_ref, wq_ref, wqs_ref, gkvl_ref, wk_ref, wv_ref,
                     gq_ref, gqs_ref, gk_ref, gks_ref, q_ref, kt_ref, v_ref):
    um = um_ref[0]
    cq = um[:, 0:MLA_Q_LORA]
    ckv = um[:, MLA_Q_LORA:MLA_Q_LORA + MLA_KV_LORA]
    kr = um[:, MLA_Q_LORA + MLA_KV_LORA:MLA_Q_LORA + MLA_KV_LORA + LANES]
    krs = um[:, MLA_Q_LORA + MLA_KV_LORA + LANES:MLA_PAD_IN]
    cqn = (cq * lax.rsqrt(jnp.mean(cq * cq, axis=-1, keepdims=True) + RMS_EPS) * gql_ref[...]).astype(BF16)
    ckvn = (ckv * lax.rsqrt(jnp.mean(ckv * ckv, axis=-1, keepdims=True) + RMS_EPS) * gkvl_ref[...]).astype(BF16)
    q = jnp.dot(cqn, wq_ref[...], preferred_element_type=F32)
    qs = jnp.dot(cqn, wqs_ref[...], preferred_element_type=F32)
    kn = jnp.dot(ckvn, wk_ref[...], preferred_element_type=F32)
    vv = jnp.dot(ckvn, wv_ref[...], preferred_element_type=F32)
    cos_t = c_ref[0]
    sin_t = s_ref[0]
    lane = lax.broadcasted_iota(I32, cos_t.shape, 1)
    scale = float(MLA_QK) ** -0.5 * _LOG2E
    inv_d = 1.0 / MLA_QK
    cq = cos_t * (gq_ref[...] * scale)
    sq = sin_t * (gqs_ref[...] * scale)
    ck = cos_t * gk_ref[...]
    k_rot = krs * (sin_t * gks_ref[...])
    for h in range(MLA_HEADS):
        sl = slice(h * HEAD_PAD, (h + 1) * HEAD_PAD)
        qh = q[:, sl]
        r = lax.rsqrt(jnp.sum(qh * qh, axis=-1, keepdims=True) * inv_d + RMS_EPS)
        q_ref[0, :, sl] = ((qh * cq + qs[:, sl] * sq) * r).astype(BF16)
        kh = kn[:, sl] + kr
        r = lax.rsqrt(jnp.sum(kh * kh, axis=-1, keepdims=True) * inv_d + RMS_EPS)
        kf = (kh * ck + k_rot) * r
        kt_ref[0, h] = kf.T.astype(BF16)
        v_ref[0, h] = jnp.where(lane == MLA_V, 1.0, vv[:, sl]).astype(BF16)


def _mla_prep(um3, cos_t, sin_t, p):
    b, t, _ = um3.shape
    tt = min(512, t)
    full = lambda shape: pl.BlockSpec(shape, lambda bi, i: (0,) * len(shape))
    tab = pl.BlockSpec((1, tt, LANES), lambda bi, i: (bi, i, 0))
    return pl.pallas_call(
        _mla_prep_kernel,
        grid=(b, t // tt),
        in_specs=[pl.BlockSpec((1, tt, MLA_PAD_IN), lambda bi, i: (bi, i, 0)), tab, tab,
                  full((1, MLA_Q_LORA)), full((MLA_Q_LORA, MLA_HEADS * HEAD_PAD)),
                  full((MLA_Q_LORA, MLA_HEADS * HEAD_PAD)),
                  full((1, MLA_KV_LORA)), full((MLA_KV_LORA, MLA_HEADS * HEAD_PAD)),
                  full((MLA_KV_LORA, MLA_HEADS * HEAD_PAD)),
                  full((1, LANES)), full((1, LANES)), full((1, LANES)), full((1, LANES))],
        out_specs=[pl.BlockSpec((1, tt, MLA_HEADS * HEAD_PAD), lambda bi, i: (bi, i, 0)),
                   pl.BlockSpec((1, MLA_HEADS, HEAD_PAD, tt), lambda bi, i: (bi, 0, 0, i)),
                   pl.BlockSpec((1, MLA_HEADS, tt, HEAD_PAD), lambda bi, i: (bi, 0, i, 0))],
        out_shape=[jax.ShapeDtypeStruct((b, t, MLA_HEADS * HEAD_PAD), BF16),
                   jax.ShapeDtypeStruct((b, MLA_HEADS, HEAD_PAD, t), BF16),
                   jax.ShapeDtypeStruct((b, MLA_HEADS, t, HEAD_PAD), BF16)],
        compiler_params=_cp(2),
        name="mla_prep",
    )(um3, cos_t, sin_t, p["gql"], p["wq"], p["wqs"], p["gkvl"], p["wk"], p["wv"],
      p["gq"], p["gqs"], p["gk"], p["gks"])


_ATTN_HEAD_GROUP = 4
_ATTN_STRIP = 512
_LOG2E = 1.4426950408889634


def _attn_kernel(q_ref, kt_ref, v_ref, o_ref, *, tq, tk):
    i = pl.program_id(1)
    nfull = i * (tq // tk)
    row = lax.broadcasted_iota(I32, (tq, tk), 0)
    col = lax.broadcasted_iota(I32, (tq, tk), 1)
    outs = []

    def step(h, off, m, acc, mask):
        q = q_ref[0, :, h * HEAD_PAD:(h + 1) * HEAD_PAD]
        s = jnp.dot(q, kt_ref[0, h, :, pl.ds(off, tk)], preferred_element_type=F32)
        ms, ps, alphas = [], [], []
        for r0 in range(0, tq, _ATTN_STRIP):
            rows = slice(r0, r0 + _ATTN_STRIP)
            s_r = s[rows]
            if mask is not None:
                s_r = jnp.where(mask[rows], s_r, _NEG)
            m_old = m[rows]
            m_new = jnp.maximum(m_old, jnp.max(s_r, axis=-1, keepdims=True))
            ps.append(jnp.exp2(s_r - m_new).astype(BF16))
            alphas.append(jnp.exp2(m_old - m_new))
            ms.append(m_new)
        p = jnp.concatenate(ps, axis=0)
        acc = acc * jnp.concatenate(alphas, axis=0) + jnp.dot(p, v_ref[0, h, pl.ds(off, tk), :],
                                                              preferred_element_type=F32)
        return jnp.concatenate(ms, axis=0), acc

    for h0 in range(0, MLA_HEADS, _ATTN_HEAD_GROUP):
        heads = range(h0, h0 + _ATTN_HEAD_GROUP)

        def body(kb, carry):
            off = pl.multiple_of(kb * tk, tk)
            return tuple(step(h, off, carry[n][0], carry[n][1], None) for n, h in enumerate(heads))

        init = tuple((jnp.full((tq, 1), _NEG, F32), jnp.zeros((tq, HEAD_PAD), F32)) for _ in heads)
        carry = lax.fori_loop(0, nfull, body, init)
        for d in range(tq // tk):
            off = pl.multiple_of(i * tq + d * tk, tk)
            carry = tuple(step(h, off, carry[n][0], carry[n][1], col + d * tk <= row)
                          for n, h in enumerate(heads))
        for m, acc in carry:
            outs.append(acc[:, :MLA_V] / acc[:, MLA_V:MLA_V + 1])
    o_ref[0] = jnp.concatenate(outs, axis=-1)


def _attention(q, kt, v):
    b, t, _ = q.shape
    tq = min(512, t)
    tk = min(512, t)
    return pl.pallas_call(
        functools.partial(_attn_kernel, tq=tq, tk=tk),
        grid=(b, t // tq),
        in_specs=[pl.BlockSpec((1, tq, MLA_HEADS * HEAD_PAD), lambda bi, i: (bi, i, 0)),
                  pl.BlockSpec((1, MLA_HEADS, HEAD_PAD, t), lambda bi, i: (bi, 0, 0, 0)),
                  pl.BlockSpec((1, MLA_HEADS, t, HEAD_PAD), lambda bi, i: (bi, 0, 0, 0))],
        out_specs=pl.BlockSpec((1, tq, MLA_OUT), lambda bi, i: (bi, i, 0)),
        out_shape=jax.ShapeDtypeStruct((b, t, MLA_OUT), F32),
        compiler_params=_cp(2),
        name="mla_attention",
    )(q, kt, v)


def _rwkv_streams(u_ref, up_ref, mu_ref, w0_ref, wup_ref, a0_ref, aup_ref, gup_ref, kk_ref, ka_ref,
                  rk_ref, ones_ref):
    i = pl.program_id(1)
    u = u_ref[0]
    prow = jnp.where(i > 0, up_ref[0][7:8, :], 0.0)
    rowi = lax.broadcasted_iota(I32, u.shape, 0)
    prev = jnp.where(rowi == 0, prow, pltpu.roll(u, 1, axis=0))
    uf = u + (prev - u) * mu_ref[...]
    c = RWKV_CH
    r = uf[:, 0:c]
    k = uf[:, c:2 * c]
    v = uf[:, 2 * c:3 * c]
    wa = uf[:, 3 * c:3 * c + LANES]
    gd = uf[:, 3 * c + LANES:]
    ones_bd = ones_ref[...]
    wl = w0_ref[...] + _mm(jnp.tanh(wa), wup_ref[...])
    z = -wl
    softplus = jnp.maximum(z, 0.0) + jnp.log(1.0 + jnp.exp(-jnp.abs(z)))
    lw = -jnp.exp(-softplus - 0.5)
    a_s = _sigmoid(a0_ref[...] + _mm(wa, aup_ref[...]))
    g = _mm(_sigmoid(gd), gup_ref[...])
    kk = k * kk_ref[...]
    nrm = jnp.maximum(jnp.sqrt(_seg_sum(kk * kk, ones_bd)), 1e-12)
    kkn = kk / nrm
    k2 = k * (1.0 + (a_s - 1.0) * ka_ref[...])
    bonus = _seg_sum(r * k2 * rk_ref[...], ones_bd) * v
    return r, lw, k2, v, -kkn, kkn * a_s, bonus, g


def _intra_chunks(rs, lws, ks, vs, as_, bs, mask_bd, eye_bd, tril, strict, incl, eye_p):
    c = CHUNK
    each = lambda fn, *lists: [fn(*xs) for xs in zip(*lists)]

    def bd(x):
        xb = x.astype(BF16)
        return jnp.where(mask_bd, jnp.concatenate([xb, xb, xb, xb], axis=0), jnp.zeros((), BF16))

    def prefix(lw):
        hi = lw.astype(BF16)
        rem = lw - hi.astype(F32)
        mid = rem.astype(BF16)
        lo = (rem - mid.astype(F32)).astype(BF16)
        return (jnp.dot(tril, hi, preferred_element_type=F32) + jnp.dot(tril, mid, preferred_element_type=F32)
                + jnp.dot(tril, lo, preferred_element_type=F32))

    cums = each(prefix, lws)
    tots = each(lambda cum: cum[c - 1:c, :], cums)
    e_negs = each(lambda cum: jnp.exp(-cum), cums)
    e_ends = each(lambda tot, cum: jnp.exp(tot - cum), tots, cums)
    rts = each(lambda r, cum: r * jnp.exp(cum), rs, cums)
    ats = each(lambda a, cum, lw: a * jnp.exp(cum - lw), as_, cums, lws)
    stacks = each(lambda at, rt: jnp.concatenate([at, rt], axis=0), ats, rts)
    s1s = each(lambda st, b, e: _mm_nt(st, bd(b * e)), stacks, bs, e_negs)
    s2s = each(lambda st, k, e: _mm_nt(st, bd(k * e)), stacks, ks, e_negs)
    a_abs = each(lambda s1: jnp.where(strict, s1[:c], 0.0), s1s)
    m_rbs = each(lambda s1: jnp.where(incl, s1[c:], 0.0), s1s)
    a_aks = each(lambda s2: jnp.where(strict, s2[:c], 0.0), s2s)
    m_rks = each(lambda s2: jnp.where(incl, s2[c:], 0.0), s2s)
    xs = each(lambda a_ab: eye_p + a_ab, a_abs)
    aks = each(lambda a_ab: _mm(a_ab, bd(a_ab)), a_abs)
    for j in range(5):
        if j < 4:
            pps = each(lambda ak, x: _mm(jnp.concatenate([ak, x], axis=0), bd(ak)), aks, xs)
            xs = each(lambda x, pp: x + pp[c:], xs, pps)
            aks = each(lambda pp: pp[:c], pps)
        else:
            xs = each(lambda x, ak: x + _mm(x, bd(ak)), xs, aks)
    bdvs = each(bd, vs)
    avs = each(_mm, a_aks, bdvs)
    ahs = each(lambda x, at: _mm(x, bd(at)), xs, ats)
    uhs = each(lambda x, av: _mm(x, bd(av)), xs, avs)
    qs = each(lambda rt, m_rb, ah: rt + _mm(m_rb, bd(ah)), rts, m_rbs, ahs)
    y0s = each(lambda m_rb, uh, m_rk, bdv: _mm(m_rb, bd(uh)) + _mm(m_rk, bdv), m_rbs, uhs, m_rks, bdvs)
    gs = each(lambda b, e, ah, tot: jnp.where(mask_bd, _mm_tn(b * e, ah), 0.0) + jnp.where(eye_bd, jnp.exp(tot), 0.0),
              bs, e_ends, ahs, tots)
    fs = each(lambda b, k, e, uh, v: jnp.where(mask_bd, _mm_tn(b * e, uh) + _mm_tn(k * e, v), 0.0),
              bs, ks, e_ends, uhs, vs)
    return qs, y0s, gs, fs


def _rwkv_intra_kernel(*refs, cps):
    q_o, y0_o, g_o, f_o, bonus_o, gate_o = refs[-6:]
    r, lw, k, v, a, b, bonus, gate = _rwkv_streams(*refs[:-6])
    bonus_o[0] = bonus
    gate_o[0] = gate
    c = CHUNK
    n = RWKV_CH
    ri = lax.broadcasted_iota(I32, (n, n), 0)
    ci = lax.broadcasted_iota(I32, (n, n), 1)
    mask_bd = (ri // RWKV_HEAD) == (ci // RWKV_HEAD)
    eye_bd = ri == ci
    t64 = lax.broadcasted_iota(I32, (c, c), 0)
    i64 = lax.broadcasted_iota(I32, (c, c), 1)
    tril = jnp.where(i64 <= t64, 1.0, 0.0).astype(BF16)
    tp = lax.broadcasted_iota(I32, (c, n), 0)
    ip = lax.broadcasted_iota(I32, (c, n), 1) & (RWKV_HEAD - 1)
    strict = ip < tp
    incl = ip <= tp
    eye_p = jnp.where(ip == tp, 1.0, 0.0)
    sls = [slice(s * c, (s + 1) * c) for s in range(cps)]
    split = lambda val: [val[sl] for sl in sls]
    qs, y0s, gs, fs = _intra_chunks(split(r), split(lw), split(k), split(v), split(a), split(b),
                                    mask_bd, eye_bd, tril, strict, incl, eye_p)
    for s, sl in enumerate(sls):
        q_o[0, sl, :] = qs[s]
        y0_o[0, sl, :] = y0s[s]
        g_o[0, s] = gs[s]
        f_o[0, s] = fs[s]


def _rwkv_intra(u3, p):
    b, t, _ = u3.shape
    n = RWKV_CH
    nc = t // CHUNK
    cps = max(d for d in (1, 2, 4, 8) if nc % d == 0)
    tt = cps * CHUNK
    hb = tt // _SUBLANES
    full = lambda shape: pl.BlockSpec(shape, lambda bi, i: (0,) * len(shape))
    tok = pl.BlockSpec((1, tt, n), lambda bi, i: (bi, i, 0))
    mat = pl.BlockSpec((1, cps, n, n), lambda bi, i: (bi, i, 0, 0))
    tok_shape = jax.ShapeDtypeStruct((b, t, n), F32)
    mat_shape = jax.ShapeDtypeStruct((b, nc, n, n), F32)
    return pl.pallas_call(
        functools.partial(_rwkv_intra_kernel, cps=cps),
        grid=(b, nc // cps),
        in_specs=[pl.BlockSpec((1, tt, RWKV_IN), lambda bi, i: (bi, i, 0)),
                  pl.BlockSpec((1, _SUBLANES, RWKV_IN), lambda bi, i: (bi, jnp.maximum(i * hb - 1, 0), 0)),
                  full((1, RWKV_IN)), full((1, RWKV_CH)), full((LANES, RWKV_CH)), full((1, RWKV_CH)),
                  full((LANES, RWKV_CH)), full((LANES, RWKV_CH)), full((1, RWKV_CH)), full((1, RWKV_CH)),
                  full((1, RWKV_CH)), full((RWKV_CH, RWKV_CH))],
        out_specs=[tok, tok, mat, mat, tok, tok],
        out_shape=[tok_shape, tok_shape, mat_shape, mat_shape, tok_shape, tok_shape],
        compiler_params=_cp(2),
        name="rwkv_intra",
    )(u3, u3, p["mu"], p["w0"], p["wup"], p["a0"], p["aup"], p["gup"], p["kk"], p["ka"], p["rk"], p["ones"])


def _rwkv_seq_kernel(q_ref, y0_ref, g_ref, f_ref, o_ref, h_ref, *, nb, cs):
    c = pl.program_id(0)

    @pl.when(c == 0)
    def _():
        h_ref[...] = jnp.zeros_like(h_ref)

    for s in range(cs):
        rows = slice(s * CHUNK, (s + 1) * CHUNK)
        for b in range(nb):
            hb = h_ref[b].astype(BF16)
            o_ref[b, rows, :] = (jnp.dot(q_ref[b, rows, :].astype(BF16), hb, preferred_element_type=F32)
                                 + y0_ref[b, rows, :])
            h_ref[b] = jnp.dot(g_ref[b, s].astype(BF16), hb, preferred_element_type=F32) + f_ref[b, s]


def _rwkv_seq(q, y0, g, f):
    b, t, n = q.shape
    nc = t // CHUNK
    cs = max(d for d in (1, 2, 4) if nc % d == 0)
    tok = pl.BlockSpec((b, cs * CHUNK, n), lambda c: (0, c, 0))
    mat = pl.BlockSpec((b, cs, n, n), lambda c: (0, c, 0, 0))
    return pl.pallas_call(
        functools.partial(_rwkv_seq_kernel, nb=b, cs=cs),
        grid=(nc // cs,),
        in_specs=[tok, tok, mat, mat],
        out_specs=tok,
        out_shape=jax.ShapeDtypeStruct((b, t, n), F32),
        scratch_shapes=[pltpu.VMEM((b, n, n), F32)],
        compiler_params=_cp(1),
        name="rwkv_seq",
    )(q, y0, g, f)


def _store_token_tiles(ref, val, tm):
    for p in range(D_MODEL // LANES):
        ref[pl.ds(p, tm, stride=_ROW_TILE), :] = val[:, p * LANES:(p + 1) * LANES]


def _load_token_tiles(ref, tm):
    return jnp.concatenate([ref[pl.ds(p, tm, stride=_ROW_TILE), :] for p in range(D_MODEL // LANES)], axis=-1)


def _out_proj_kernel(x_ref, yc_ref, ym_ref, yw_ref, bonus_ref, gate_ref, lng_ref, lnb_ref, ones_ref,
                     w_ref, gt_ref, g2_ref, sc_ref, sh_ref, *rest, with_router, tm):
    if with_router:
        wr_ref, xo_ref, h_ref, lg_ref = rest
    else:
        xo_ref, h_ref = rest
    ones_bd = ones_ref[...]
    inv = 1.0 / RWKV_HEAD
    y = yw_ref[...]
    mu = _seg_sum(y, ones_bd) * inv
    yc = y - mu
    var = _seg_sum(yc * yc, ones_bd) * inv
    yn = yc * lax.rsqrt(var + RWKV_GN_EPS) * lng_ref[...] + lnb_ref[...]
    yr = (yn + bonus_ref[...]) * gate_ref[...]
    mixed = (_mm(yc_ref[...], w_ref[0:CONV_CH, :])
             + _mm(ym_ref[...], w_ref[CONV_CH:CONV_CH + MLA_OUT, :])
             + _mm(yr, w_ref[CONV_CH + MLA_OUT:, :]))
    x = x_ref[...] + gt_ref[0] * mixed
    xo_ref[...] = x
    ms = jnp.mean(x * x, axis=-1, keepdims=True)
    h = x * lax.rsqrt(ms + RMS_EPS) * g2_ref[...]
    h = h * (1.0 + sc_ref[0]) + sh_ref[0]
    if with_router:
        _store_token_tiles(h_ref, h, tm)
    else:
        h_ref[...] = h
    if with_router:
        h_hi, h_lo = _split2(h)
        w_hi, w_lo = _split2(wr_ref[...])
        lg_ref[...] = (jnp.dot(h_hi, w_hi, preferred_element_type=F32)
                       + jnp.dot(h_lo, w_hi, preferred_element_type=F32)
                       + jnp.dot(h_hi, w_lo, preferred_element_type=F32))


def _out_proj(x2, yc, ym, yw, bonus, gate, ln_g, ln_b, ones_bd, w_out, gt, g2, sc, sh, router, t):
    n = x2.shape[0]
    tm = min(512, t)
    tpb = t // tm
    vec = pl.BlockSpec((1, 1, D_MODEL), lambda i: (i // tpb, 0, 0))
    row = lambda w: pl.BlockSpec((tm, w), lambda i: (i, 0))
    full = lambda r, c: pl.BlockSpec((r, c), lambda i: (0, 0))
    in_specs = [row(D_MODEL), row(CONV_CH), row(MLA_OUT), row(RWKV_CH), row(RWKV_CH), row(RWKV_CH),
                full(1, RWKV_CH), full(1, RWKV_CH), full(RWKV_CH, RWKV_CH),
                full(D_MODEL, D_MODEL), vec, full(1, D_MODEL), vec, vec]
    out_specs = [row(D_MODEL), row(D_MODEL)]
    out_shape = [jax.ShapeDtypeStruct((n, D_MODEL), F32), jax.ShapeDtypeStruct((n, D_MODEL), F32)]
    args = [x2, yc, ym, yw, bonus, gate, ln_g, ln_b, ones_bd, w_out, gt, g2, sc, sh]
    if router is not None:
        in_specs.append(full(D_MODEL, LANES))
        out_specs = [row(D_MODEL), pl.BlockSpec((tm * _ROW_TILE, LANES), lambda i: (i, 0)), row(LANES)]
        out_shape = [out_shape[0], jax.ShapeDtypeStruct((n * _ROW_TILE, LANES), F32),
                     jax.ShapeDtypeStruct((n, LANES), F32)]
        args.append(router)
    return pl.pallas_call(
        functools.partial(_out_proj_kernel, with_router=router is not None, tm=tm),
        grid=(n // tm,),
        in_specs=in_specs, out_specs=out_specs, out_shape=out_shape,
        compiler_params=_cp(1),
        name="out_proj",
    )(*args)


def _swiglu_step(load_h, wg_ref, wu_ref, wd_ref, hb_ref, acc_ref, j):
    @pl.when(j == 0)
    def _():
        hb_ref[...] = load_h().astype(BF16)
        acc_ref[...] = jnp.zeros_like(acc_ref)

    hb = hb_ref[...]
    gate = jnp.dot(hb, wg_ref[...], preferred_element_type=F32)
    up = jnp.dot(hb, wu_ref[...], preferred_element_type=F32)
    act = (gate * _sigmoid(gate) * up).astype(BF16)
    acc_ref[...] += jnp.dot(act, wd_ref[...], preferred_element_type=F32)


def _ffn_kernel(h_ref, x_ref, gt_ref, wg_ref, wu_ref, wd_ref, o_ref, hb_ref, acc_ref, *, nj):
    j = pl.program_id(1)
    _swiglu_step(lambda: h_ref[...], wg_ref, wu_ref, wd_ref, hb_ref, acc_ref, j)

    @pl.when(j == nj - 1)
    def _():
        o_ref[...] = x_ref[...] + gt_ref[0] * acc_ref[...]


def _dense_ffn(h2, x2, gt, wg, wu, wd, t):
    n = x2.shape[0]
    ff = wg.shape[1]
    tm = min(512, t)
    tpb = t // tm
    nj = 2
    tf = ff // nj
    row = pl.BlockSpec((tm, D_MODEL), lambda i, j: (i, 0))
    return pl.pallas_call(
        functools.partial(_ffn_kernel, nj=nj),
        grid=(n // tm, nj),
        in_specs=[row, row, pl.BlockSpec((1, 1, D_MODEL), lambda i, j: (i // tpb, 0, 0)),
                  pl.BlockSpec((D_MODEL, tf), lambda i, j: (0, j)),
                  pl.BlockSpec((D_MODEL, tf), lambda i, j: (0, j)),
                  pl.BlockSpec((tf, D_MODEL), lambda i, j: (j, 0))],
        out_specs=row,
        out_shape=jax.ShapeDtypeStruct((n, D_MODEL), F32),
        scratch_shapes=[pltpu.VMEM((tm, D_MODEL), BF16), pltpu.VMEM((tm, D_MODEL), F32)],
        compiler_params=_cp(2),
        name="dense_ffn",
    )(h2, x2, gt, wg, wu, wd)


def _moe_ffn_kernel(te_ref, nu_ref, xs_ref, wg_ref, wu_ref, wd_ref, o_ref, hb_ref, acc_ref, *, nj, tm):
    i = pl.program_id(0)
    j = pl.program_id(1)
    used = i < nu_ref[0]

    @pl.when(used)
    def _():
        _swiglu_step(lambda: _load_token_tiles(xs_ref, tm), wg_ref.at[0], wu_ref.at[0], wd_ref.at[0],
                     hb_ref, acc_ref, j)

    @pl.when(j == nj - 1)
    def _():
        _store_token_tiles(o_ref, jnp.where(used, acc_ref[...], 0.0), tm)


def _moe_ffn(xs, tile_e, n_used, wg, wu, wd):
    cap = xs.shape[0] // _ROW_TILE
    ff = wg.shape[2]
    nj = 2
    tf = ff // nj
    tm = MOE_TILE

    def wj(i, j, nu):
        return jnp.where(i < nu[0], j, nj - 1)

    row = pl.BlockSpec((tm * _ROW_TILE, LANES), lambda i, j, te, nu: (i, 0))
    grid_spec = pltpu.PrefetchScalarGridSpec(
        num_scalar_prefetch=2,
        grid=(cap // tm, nj),
        in_specs=[row,
                  pl.BlockSpec((1, D_MODEL, tf), lambda i, j, te, nu: (te[i], 0, wj(i, j, nu))),
                  pl.BlockSpec((1, D_MODEL, tf), lambda i, j, te, nu: (te[i], 0, wj(i, j, nu))),
                  pl.BlockSpec((1, tf, D_MODEL), lambda i, j, te, nu: (te[i], wj(i, j, nu), 0))],
        out_specs=row,
        scratch_shapes=[pltpu.VMEM((tm, D_MODEL), BF16), pltpu.VMEM((tm, D_MODEL), F32)])
    return pl.pallas_call(
        functools.partial(_moe_ffn_kernel, nj=nj, tm=tm),
        grid_spec=grid_spec,
        out_shape=jax.ShapeDtypeStruct((cap * _ROW_TILE, LANES), F32),
        compiler_params=_cp(2),
        name="moe_ffn",
    )(tile_e, n_used, xs, wg, wu, wd)


def _route_kernel(lg_ref, oi_ref, of_ref, cnt_ref, carry_ref, *, tm):
    i = pl.program_id(0)

    @pl.when(i == 0)
    def _():
        carry_ref[...] = jnp.zeros_like(carry_ref)

    lane = lax.broadcasted_iota(I32, (tm, LANES), 1)
    real = lane < N_EXPERTS
    lg = jnp.where(real, lg_ref[...], _NEG)
    ex = jnp.exp(lg - jnp.max(lg, axis=-1, keepdims=True))
    p = ex / jnp.sum(ex, axis=-1, keepdims=True)
    p = jnp.where(real, p, -1.0)
    m1 = jnp.max(p, axis=-1, keepdims=True)
    i1 = jnp.min(jnp.where(p == m1, lane, LANES), axis=-1, keepdims=True)
    p2 = jnp.where(lane == i1, -1.0, p)
    m2 = jnp.max(p2, axis=-1, keepdims=True)
    i2 = jnp.min(jnp.where(p2 == m2, lane, LANES), axis=-1, keepdims=True)
    den = m1 + m2
    oh1 = lane == i1
    oh2 = lane == i2
    oh = jnp.where(oh1 | oh2, 1.0, 0.0)
    rr = lax.broadcasted_iota(I32, (tm, tm), 0)
    cc = lax.broadcasted_iota(I32, (tm, tm), 1)
    lstrict = jnp.where(cc < rr, 1.0, 0.0).astype(BF16)
    pre = jnp.dot(lstrict, oh.astype(BF16), preferred_element_type=F32) + carry_ref[...]
    rank1 = jnp.sum(jnp.where(oh1, pre, 0.0), axis=-1, keepdims=True).astype(I32)
    rank2 = jnp.sum(jnp.where(oh2, pre, 0.0), axis=-1, keepdims=True).astype(I32)
    carry_ref[...] = carry_ref[...] + jnp.sum(oh, axis=0, keepdims=True)
    cnt_ref[...] = carry_ref[...]
    oi_ref[...] = jnp.where(lane == 0, i1, jnp.where(lane == 1, i2, jnp.where(lane == 2, rank1,
                                                                                jnp.where(lane == 3, rank2, 0))))
    of_ref[...] = jnp.where(lane == 0, m1 / den, jnp.where(lane == 1, m2 / den, 0.0))


def _route(logits, t):
    n = logits.shape[0]
    tm = min(512, t)
    row = pl.BlockSpec((tm, LANES), lambda i: (i, 0))
    return pl.pallas_call(
        functools.partial(_route_kernel, tm=tm),
        grid=(n // tm,),
        in_specs=[row],
        out_specs=[row, row, pl.BlockSpec((1, LANES), lambda i: (0, 0))],
        out_shape=[jax.ShapeDtypeStruct((n, LANES), I32), jax.ShapeDtypeStruct((n, LANES), F32),
                   jax.ShapeDtypeStruct((1, LANES), F32)],
        scratch_shapes=[pltpu.VMEM((1, LANES), F32)],
        compiler_params=_cp(1),
        name="moe_route",
    )(logits)


def _dispatch_kernel(s0_ref, s1_ref, h_ref, init_ref, xs_ref, sem, *, tm):
    del init_ref
    base = pl.program_id(0) * tm

    def row_copy(r, slot):
        return pltpu.make_async_copy(h_ref.at[pl.ds(pl.multiple_of(r * _ROW_TILE, _ROW_TILE), _ROW_TILE)],
                                     xs_ref.at[pl.ds(pl.multiple_of(slot * _ROW_TILE, _ROW_TILE), _ROW_TILE)],
                                     sem)

    def start(r, carry):
        row_copy(r, s0_ref[base + r]).start(priority=0)
        row_copy(r, s1_ref[base + r]).start(priority=1)
        return carry

    lax.fori_loop(0, tm, start, 0)

    def wait(r, carry):
        row_copy(r, s0_ref[base + r]).wait()
        row_copy(r, s1_ref[base + r]).wait()
        return carry

    lax.fori_loop(0, tm, wait, 0)


def _dispatch(h2, slot0, slot1, cap, t):
    n = h2.shape[0] // _ROW_TILE
    tm = min(512, t)
    grid_spec = pltpu.PrefetchScalarGridSpec(
        num_scalar_prefetch=2,
        grid=(n // tm,),
        in_specs=[pl.BlockSpec((tm * _ROW_TILE, LANES), lambda i, s0, s1: (i, 0)),
                  pl.BlockSpec(memory_space=pl.ANY)],
        out_specs=pl.BlockSpec(memory_space=pl.ANY),
        scratch_shapes=[pltpu.SemaphoreType.DMA(())])
    return pl.pallas_call(
        functools.partial(_dispatch_kernel, tm=tm),
        grid_spec=grid_spec,
        out_shape=jax.ShapeDtypeStruct((cap * _ROW_TILE, LANES), F32),
        input_output_aliases={3: 0},
        compiler_params=_cp(1),
        name="moe_dispatch",
    )(slot0, slot1, h2, jnp.zeros((cap * _ROW_TILE, LANES), F32))


def _combine_kernel(s0_ref, s1_ref, x_ref, wf_ref, gt_ref, ys_ref, o_ref, buf0, buf1, sem, *, tm):
    base = pl.program_id(0) * tm

    def row_copy(slot, buf, r):
        return pltpu.make_async_copy(ys_ref.at[pl.ds(pl.multiple_of(slot * _ROW_TILE, _ROW_TILE), _ROW_TILE)],
                                     buf.at[pl.ds(pl.multiple_of(r * _ROW_TILE, _ROW_TILE), _ROW_TILE)], sem)

    def start(r, carry):
        row_copy(s0_ref[base + r], buf0, r).start(priority=0)
        row_copy(s1_ref[base + r], buf1, r).start(priority=1)
        return carry

    lax.fori_loop(0, tm, start, 0)

    def wait(r, carry):
        row_copy(s0_ref[base + r], buf0, r).wait()
        row_copy(s1_ref[base + r], buf1, r).wait()
        return carry

    lax.fori_loop(0, tm, wait, 0)
    wf = wf_ref[...]
    f = wf[:, 0:1] * _load_token_tiles(buf0, tm) + wf[:, 1:2] * _load_token_tiles(buf1, tm)
    o_ref[...] = x_ref[...] + gt_ref[0] * f


def _combine(x2, ys, slot0, slot1, wf, gt, t):
    n = x2.shape[0]
    tm = min(512, t)
    tpb = t // tm
    row = pl.BlockSpec((tm, D_MODEL), lambda i, s0, s1: (i, 0))
    grid_spec = pltpu.PrefetchScalarGridSpec(
        num_scalar_prefetch=2,
        grid=(n // tm,),
        in_specs=[row, pl.BlockSpec((tm, LANES), lambda i, s0, s1: (i, 0)),
                  pl.BlockSpec((1, 1, D_MODEL), lambda i, s0, s1: (i // tpb, 0, 0)),
                  pl.BlockSpec(memory_space=pl.ANY)],
        out_specs=row,
        scratch_shapes=[pltpu.VMEM((tm * _ROW_TILE, LANES), F32), pltpu.VMEM((tm * _ROW_TILE, LANES), F32),
                        pltpu.SemaphoreType.DMA(())])
    return pl.pallas_call(
        functools.partial(_combine_kernel, tm=tm),
        grid_spec=grid_spec,
        out_shape=jax.ShapeDtypeStruct((n, D_MODEL), F32),
        compiler_params=_cp(1),
        name="moe_combine",
    )(slot0, slot1, x2, wf, gt, ys)


def _moe(h2, x2, logits, gt, wg, wu, wd, t):
    n = x2.shape[0]
    info_i, info_f, cnt = _route(logits, t)
    counts = cnt[0, :N_EXPERTS].astype(I32)
    padded = (counts + MOE_TILE - 1) // MOE_TILE * MOE_TILE
    pad_ends = jnp.cumsum(padded)
    pad_starts = pad_ends - padded
    slot0 = pad_starts[info_i[:, 0]] + info_i[:, 2]
    slot1 = pad_starts[info_i[:, 1]] + info_i[:, 3]
    cap = (2 * n // MOE_TILE + N_EXPERTS) * MOE_TILE
    n_tiles = cap // MOE_TILE
    n_used = (pad_ends[-1:] // MOE_TILE).astype(I32)
    tile_e = jnp.minimum(jnp.searchsorted(pad_ends, jnp.arange(n_tiles, dtype=I32) * MOE_TILE, side="right"),
                         N_EXPERTS - 1).astype(I32)
    last_e = tile_e[jnp.maximum(n_used[0] - 1, 0)]
    tile_e = jnp.where(jnp.arange(n_tiles) < n_used[0], tile_e, last_e)
    xs = _dispatch(h2, slot0, slot1, cap, t)
    ys = _moe_ffn(xs, tile_e, n_used, wg, wu, wd)
    return _combine(x2, ys, slot0, slot1, info_f, gt, t)


def _cast_kernel(a_ref, b_ref, c_ref, ao_ref, bo_ref, co_ref):
    ao_ref[...] = a_ref[...].astype(BF16)
    bo_ref[...] = b_ref[...].astype(BF16)
    co_ref[...] = c_ref[...].astype(BF16)


def _expert_weights_bf16(layer, w_gate, w_up, w_down):
    _, ne, d, ff = w_gate.shape
    rows = 256
    gu = pl.BlockSpec((None, 1, rows, ff), lambda e, r: (layer, e, r, 0))
    dn = pl.BlockSpec((None, 1, rows * ff // d, d), lambda e, r: (layer, e, r, 0))
    gu_o = pl.BlockSpec((1, rows, ff), lambda e, r: (e, r, 0))
    dn_o = pl.BlockSpec((1, rows * ff // d, d), lambda e, r: (e, r, 0))
    return pl.pallas_call(
        _cast_kernel,
        grid=(ne, d // rows),
        in_specs=[gu, gu, dn],
        out_specs=[gu_o, gu_o, dn_o],
        out_shape=[jax.ShapeDtypeStruct((ne, d, ff), BF16), jax.ShapeDtypeStruct((ne, d, ff), BF16),
                   jax.ShapeDtypeStruct((ne, ff, d), BF16)],
        compiler_params=_cp(2),
        name="expert_weights_bf16",
    )(w_gate, w_up, w_down)


def _pad_last(a, width):
    return jnp.pad(a, [(0, 0)] * (a.ndim - 1) + [(0, width - a.shape[-1])])


def _swap_rope(a):
    h = MLA_ROPE // 2
    return jnp.concatenate([a[..., :MLA_NOPE], a[..., MLA_NOPE + h:], a[..., MLA_NOPE:MLA_NOPE + h]], axis=-1)


def _layer_params(l, w_in, mla_q_norm_g, mla_w_uq, mla_kv_norm_g, mla_w_ukv, qk_norm_q, qk_norm_k,
                  rwkv_mu, rwkv_w0, rwkv_w_up, rwkv_a0, rwkv_a_up, rwkv_g_up, rwkv_k_k, rwkv_k_a, rwkv_r_k):
    w = w_in[l]
    m0 = CONV_IN
    kr0 = m0 + MLA_Q_LORA + MLA_KV_LORA
    kr = w[:, kr0:kr0 + MLA_ROPE]
    h = MLA_ROPE // 2
    zeros = lambda c: jnp.zeros((D_MODEL, c), F32)
    kr_tile = jnp.concatenate([zeros(MLA_NOPE), kr, zeros(LANES - MLA_QK)], axis=1)
    krs_tile = jnp.concatenate([zeros(MLA_NOPE), kr[:, h:], kr[:, :h], zeros(LANES - MLA_QK)], axis=1)
    w_in_p = jnp.concatenate([w[:, :kr0], kr_tile, krs_tile, w[:, kr0 + MLA_ROPE:]], axis=1).astype(BF16)

    wq = mla_w_uq[l].reshape(MLA_Q_LORA, MLA_HEADS, MLA_QK)
    wkv = mla_w_ukv[l].reshape(MLA_KV_LORA, MLA_HEADS, MLA_NOPE + MLA_V)
    flat = lambda a: _pad_last(a, HEAD_PAD).reshape(a.shape[0], MLA_HEADS * HEAD_PAD).astype(BF16)
    mla = dict(
        gql=mla_q_norm_g[l].reshape(1, -1), gkvl=mla_kv_norm_g[l].reshape(1, -1),
        wq=flat(wq), wqs=flat(_swap_rope(wq)),
        wk=flat(wkv[..., :MLA_NOPE]), wv=flat(wkv[..., MLA_NOPE:]),
        gq=_pad_last(qk_norm_q[l], LANES).reshape(1, LANES),
        gqs=_pad_last(_swap_rope(qk_norm_q[l]), LANES).reshape(1, LANES),
        gk=_pad_last(qk_norm_k[l], LANES).reshape(1, LANES),
        gks=_pad_last(_swap_rope(qk_norm_k[l]), LANES).reshape(1, LANES))

    zl = jnp.zeros((LANES // 2, RWKV_CH), F32)
    hid = jnp.arange(RWKV_CH) // RWKV_HEAD
    rwkv = dict(
        mu=rwkv_mu[l].reshape(1, -1), w0=rwkv_w0[l].reshape(1, -1), a0=rwkv_a0[l].reshape(1, -1),
        wup=jnp.concatenate([rwkv_w_up[l], zl], axis=0).astype(BF16),
        aup=jnp.concatenate([zl, rwkv_a_up[l]], axis=0).astype(BF16),
        gup=rwkv_g_up[l].astype(BF16),
        kk=rwkv_k_k[l].reshape(1, -1), ka=rwkv_k_a[l].reshape(1, -1), rk=rwkv_r_k[l].reshape(1, -1),
        ones=(hid[:, None] == hid[None, :]).astype(BF16))
    return w_in_p, mla, rwkv


def kernel(x, c, positions, ada_w, ada_b, ada_layer_bias, norm1_g, norm2_g, w_in, w_out, conv_w, conv_b, conv_ln_g, conv_ln_b, mla_q_norm_g, mla_w_uq, mla_kv_norm_g, mla_w_ukv, qk_norm_q, qk_norm_k, rwkv_mu, rwkv_w0, rwkv_w_up, rwkv_a0, rwkv_a_up, rwkv_g_up, rwkv_k_k, rwkv_k_a, rwkv_r_k, rwkv_ln_g, rwkv_ln_b, ffn_w_gate, ffn_w_up, ffn_w_down, moe_router, moe_w_gate, moe_w_up, moe_w_down):
    b, t, d = x.shape
    n = b * t
    mods = _cond(c, ada_w, ada_b, ada_layer_bias)
    cos_t, sin_t = _rope_tables(positions)
    x2 = x.reshape(n, d)
    for l in range(DEPTH):
        sh1, sc1, gt1, sh2, sc2, gt2 = [mods[l, :, k * d:(k + 1) * d].reshape(b, 1, d) for k in range(N_MOD)]
        w_in_p, mla_p, rwkv_p = _layer_params(
            l, w_in, mla_q_norm_g, mla_w_uq, mla_kv_norm_g, mla_w_ukv, qk_norm_q, qk_norm_k,
            rwkv_mu, rwkv_w0, rwkv_w_up, rwkv_a0, rwkv_a_up, rwkv_g_up, rwkv_k_k, rwkv_k_a, rwkv_r_k)
        u_conv, u_mla, u_rwkv = _in_proj(x2, sc1, sh1, norm1_g[l].reshape(1, d), w_in_p, t)
        y_conv = _conformer_conv(u_conv.reshape(b, t, CONV_IN), conv_w[l], conv_b[l], conv_ln_g[l], conv_ln_b[l])
        q, kt, v = _mla_prep(u_mla.reshape(b, t, MLA_PAD_IN), cos_t, sin_t, mla_p)
        y_mla = _attention(q, kt, v)
        qc, y0, g_, f_, bonus, gate = _rwkv_intra(u_rwkv.reshape(b, t, RWKV_IN), rwkv_p)
        y_wkv = _rwkv_seq(qc, y0, g_, f_)
        i = l // 2
        router = None
        if l % 2 == 1:
            router = _pad_last(moe_router[i], LANES)
        flat = lambda a: a.reshape(n, a.shape[-1])
        outs = _out_proj(x2, flat(y_conv), flat(y_mla), flat(y_wkv), flat(bonus), flat(gate),
                         rwkv_ln_g[l].reshape(1, -1), rwkv_ln_b[l].reshape(1, -1), rwkv_p["ones"],
                         w_out[l].astype(BF16), gt1, norm2_g[l].reshape(1, d), sc2, sh2, router, t)
        if l % 2 == 0:
            x2, h2 = outs
            x2 = _dense_ffn(h2, x2, gt2, ffn_w_gate[i].astype(BF16), ffn_w_up[i].astype(BF16),
                            ffn_w_down[i].astype(BF16), t)
        else:
            x2, h2, logits = outs
            wg_b, wu_b, wd_b = _expert_weights_bf16(i, moe_w_gate, moe_w_up, moe_w_down)
            x2 = _moe(h2, x2, logits, gt2, wg_b, wu_b, wd_b, t)
    return x2.reshape(b, t, d)
```

```python
import functools

import jax
import jax.numpy as jnp
from jax import lax
from jax.experimental import pallas as pl
from jax.experimental.pallas import tpu as pltpu

F32 = jnp.float32
BF16 = jnp.bfloat16
I32 = jnp.int32

D_MODEL = 1024
DEPTH = 4
N_MOD = 6
CONV_CH = 256
CONV_K = 31
CONV_IN = 2 * CONV_CH
MLA_NOPE = 64
MLA_ROPE = 32
MLA_V = 64
MLA_QK = MLA_NOPE + MLA_ROPE
MLA_HEADS = 8
MLA_OUT = MLA_HEADS * MLA_V
MLA_Q_LORA = 256
MLA_KV_LORA = 128
MLA_IN = MLA_Q_LORA + MLA_KV_LORA + MLA_ROPE
ROPE_THETA = 10000.0
RWKV_HEAD = 64
RWKV_CH = 256
RWKV_HEADS = 4
RWKV_IN = 1024
N_EXPERTS = 8
RMS_EPS = 1e-6
LN_EPS = 1e-5
RWKV_GN_EPS = 64e-5

LANES = 128
HEAD_PAD = LANES
MLA_PAD_IN = MLA_Q_LORA + MLA_KV_LORA + 2 * LANES
IN_PAD = CONV_IN + MLA_PAD_IN + RWKV_IN
CHUNK = 64
MOE_TILE = 512
_ROW_TILE = D_MODEL // LANES
VMEM_LIMIT = 56 * 1024 * 1024

_NEG = -1e30


def _cp(n_axes, vmem=None):
    return pltpu.CompilerParams(dimension_semantics=("arbitrary",) * n_axes,
                                vmem_limit_bytes=vmem if vmem else VMEM_LIMIT)


def _mm(a, b):
    return jnp.dot(a.astype(BF16), b.astype(BF16), preferred_element_type=F32)


def _mm_nt(a, b):
    return lax.dot_general(a.astype(BF16), b.astype(BF16), (((1,), (1,)), ((), ())),
                           preferred_element_type=F32)


def _mm_tn(a, b):
    return lax.dot_general(a.astype(BF16), b.astype(BF16), (((0,), (0,)), ((), ())),
                           preferred_element_type=F32)


def _sigmoid(x):
    return 1.0 / (1.0 + jnp.exp(-x))


def _split2(x):
    hi = x.astype(BF16)
    lo = (x - hi.astype(F32)).astype(BF16)
    return hi, lo


def _seg_sum(x, ones_bd):
    hi, lo = _split2(x)
    return (jnp.dot(hi, ones_bd, preferred_element_type=F32)
            + jnp.dot(lo, ones_bd, preferred_element_type=F32))


def _cond_kernel(c_ref, w_ref, b_ref, lb_ref, o_ref):
    c = c_ref[...]
    s = c * _sigmoid(c)
    base = jnp.dot(s, w_ref[...], precision=lax.Precision.HIGHEST, preferred_element_type=F32) + b_ref[...]
    for l in range(DEPTH):
        o_ref[l] = base + lb_ref[l]


def _cond(c, ada_w, ada_b, ada_layer_bias):
    b = c.shape[0]
    n = ada_w.shape[1]
    tn = 1536
    cp = jnp.zeros((8, D_MODEL), F32).at[:b].set(c)
    out = pl.pallas_call(
        _cond_kernel,
        grid=(n // tn,),
        in_specs=[pl.BlockSpec((8, D_MODEL), lambda j: (0, 0)),
                  pl.BlockSpec((D_MODEL, tn), lambda j: (0, j)),
                  pl.BlockSpec((1, tn), lambda j: (0, j)),
                  pl.BlockSpec((DEPTH, 1, tn), lambda j: (0, 0, j))],
        out_specs=pl.BlockSpec((DEPTH, 8, tn), lambda j: (0, 0, j)),
        out_shape=jax.ShapeDtypeStruct((DEPTH, 8, n), F32),
        compiler_params=_cp(1),
        name="cond",
    )(cp, ada_w, ada_b.reshape(1, n), ada_layer_bias.reshape(DEPTH, 1, n))
    return out[:, :b]


def _rope_kernel(pos_ref, invf_ref, c_ref, s_ref):
    pos = pos_ref[0].astype(F32)
    ang = pos * invf_ref[...]
    lane = lax.broadcasted_iota(I32, ang.shape, 1)
    cs = jnp.cos(ang)
    sn = jnp.sin(ang)
    c_ref[0] = jnp.where(lane < MLA_NOPE, 1.0, jnp.where(lane < MLA_QK, cs, 0.0))
    s_ref[0] = jnp.where((lane >= MLA_NOPE) & (lane < MLA_NOPE + 16), -sn,
                         jnp.where((lane >= MLA_NOPE + 16) & (lane < MLA_QK), sn, 0.0))


def _rope_tables(positions):
    b, t = positions.shape
    tt = min(512, t)
    inv_freq = ROPE_THETA ** (-jnp.arange(0, MLA_ROPE, 2, dtype=F32) / MLA_ROPE)
    invf = jnp.zeros((1, LANES), F32)
    invf = invf.at[0, MLA_NOPE:MLA_NOPE + 16].set(inv_freq).at[0, MLA_NOPE + 16:MLA_QK].set(inv_freq)
    spec = pl.BlockSpec((1, tt, LANES), lambda bi, i: (bi, i, 0))
    return pl.pallas_call(
        _rope_kernel,
        grid=(b, t // tt),
        in_specs=[pl.BlockSpec((1, tt, 1), lambda bi, i: (bi, i, 0)),
                  pl.BlockSpec((1, LANES), lambda bi, i: (0, 0))],
        out_specs=[spec, spec],
        out_shape=[jax.ShapeDtypeStruct((b, t, LANES), F32)] * 2,
        compiler_params=_cp(2),
        name="rope_tables",
    )(positions.reshape(b, t, 1), invf)


def _in_proj_kernel(x_ref, sc_ref, sh_ref, g_ref, w_ref, oc_ref, om_ref, or_ref):
    x = x_ref[...]
    ms = jnp.mean(x * x, axis=-1, keepdims=True)
    h = x * lax.rsqrt(ms + RMS_EPS) * g_ref[...]
    h = h * (1.0 + sc_ref[0]) + sh_ref[0]
    hb = h.astype(BF16)
    oc_ref[...] = jnp.dot(hb, w_ref[:, 0:CONV_IN], preferred_element_type=F32)
    om_ref[...] = jnp.dot(hb, w_ref[:, CONV_IN:CONV_IN + MLA_PAD_IN], preferred_element_type=F32)
    or_ref[...] = jnp.dot(hb, w_ref[:, CONV_IN + MLA_PAD_IN:IN_PAD], preferred_element_type=F32)


def _in_proj(x2, sc, sh, g, w, t):
    n = x2.shape[0]
    tm = min(512, t)
    tpb = t // tm
    vec = pl.BlockSpec((1, 1, D_MODEL), lambda i: (i // tpb, 0, 0))
    return pl.pallas_call(
        _in_proj_kernel,
        grid=(n // tm,),
        in_specs=[pl.BlockSpec((tm, D_MODEL), lambda i: (i, 0)), vec, vec,
                  pl.BlockSpec((1, D_MODEL), lambda i: (0, 0)),
                  pl.BlockSpec((D_MODEL, IN_PAD), lambda i: (0, 0))],
        out_specs=[pl.BlockSpec((tm, CONV_IN), lambda i: (i, 0)),
                   pl.BlockSpec((tm, MLA_PAD_IN), lambda i: (i, 0)),
                   pl.BlockSpec((tm, RWKV_IN), lambda i: (i, 0))],
        out_shape=[jax.ShapeDtypeStruct((n, CONV_IN), F32),
                   jax.ShapeDtypeStruct((n, MLA_PAD_IN), F32),
                   jax.ShapeDtypeStruct((n, RWKV_IN), F32)],
        compiler_params=_cp(1),
        name="in_proj",
    )(x2, sc, sh, g, w)


_CONV_HALO = 32
_SUBLANES = 8


def _conv_kernel(u_ref, up_ref, w_ref, b_ref, g_ref, bb_ref, o_ref, zbuf, *, tt):
    i = pl.program_id(1)
    u = u_ref[0]
    z = u[:, :CONV_CH] * _sigmoid(u[:, CONV_CH:])
    up = up_ref[0]
    zp = up[:, :CONV_CH] * _sigmoid(up[:, CONV_CH:])
    zp = jnp.where(i > 0, zp, 0.0)
    zbuf[0:_CONV_HALO, :] = zp
    zbuf[_CONV_HALO:, :] = z
    acc = jnp.zeros((tt, CONV_CH), F32)
    first = _CONV_HALO - (CONV_K - 1)
    zfull = zbuf[...]
    nrows = tt + _CONV_HALO
    for sh in range(_SUBLANES):
        taps = [k for k in range(CONV_K) if (k + first) % _SUBLANES == sh]
        shifted = zfull if sh == 0 else pltpu.roll(zfull, nrows - sh, axis=0)
        for k in taps:
            a0 = k + first - sh
            acc = acc + shifted[a0:a0 + tt] * w_ref[k:k + 1, :]
    y = acc + b_ref[...]
    mu = jnp.mean(y, axis=-1, keepdims=True)
    yc = y - mu
    var = jnp.mean(yc * yc, axis=-1, keepdims=True)
    yn = yc * lax.rsqrt(var + LN_EPS) * g_ref[...] + bb_ref[...]
    o_ref[0] = yn * _sigmoid(yn)


def _conformer_conv(u3, conv_w, conv_b, ln_g, ln_b):
    b, t, _ = u3.shape
    tt = min(512, t)
    hb = tt // _CONV_HALO
    vec = pl.BlockSpec((1, CONV_CH), lambda bi, i: (0, 0))
    return pl.pallas_call(
        functools.partial(_conv_kernel, tt=tt),
        grid=(b, t // tt),
        in_specs=[pl.BlockSpec((1, tt, CONV_IN), lambda bi, i: (bi, i, 0)),
                  pl.BlockSpec((1, _CONV_HALO, CONV_IN), lambda bi, i: (bi, jnp.maximum(i * hb - 1, 0), 0)),
                  pl.BlockSpec((CONV_K, CONV_CH), lambda bi, i: (0, 0)), vec, vec, vec],
        out_specs=pl.BlockSpec((1, tt, CONV_CH), lambda bi, i: (bi, i, 0)),
        out_shape=jax.ShapeDtypeStruct((b, t, CONV_CH), F32),
        scratch_shapes=[pltpu.VMEM((tt + _CONV_HALO, CONV_CH), F32)],
        compiler_params=_cp(2),
        name="conformer_conv",
    )(u3, u3, conv_w, conv_b.reshape(1, -1), ln_g.reshape(1, -1), ln_b.reshape(1, -1))


def _mla_prep_kernel(um_ref, c_ref, s_ref, gql_ref, wq_ref, wqs_ref, gkvl_ref, wk_ref, wv_ref,
                     gq_ref, gqs_ref, gk_ref, gks_ref, q_ref, kt_ref, v_ref):
    um = um_ref[0]
    cq = um[:, 0:MLA_Q_LORA]
    ckv = um[:, MLA_Q_LORA:MLA_Q_LORA + MLA_KV_LORA]
    kr = um[:, MLA_Q_LORA + MLA_KV_LORA:MLA_Q_LORA + MLA_KV_LORA + LANES]
    krs = um[:, MLA_Q_LORA + MLA_KV_LORA + LANES:MLA_PAD_IN]
    cqn = (cq * lax.rsqrt(jnp.mean(cq * cq, axis=-1, keepdims=True) + RMS_EPS) * gql_ref[...]).astype(BF16)
    ckvn = (ckv * lax.rsqrt(jnp.mean(ckv * ckv, axis=-1, keepdims=True) + RMS_EPS) * gkvl_ref[...]).astype(BF16)
    q = jnp.dot(cqn, wq_ref[...], preferred_element_type=F32)
    qs = jnp.dot(cqn, wqs_ref[...], preferred_element_type=F32)
    kn = jnp.dot(ckvn, wk_ref[...], preferred_element_type=F32)
    vv = jnp.dot(ckvn, wv_ref[...], preferred_element_type=F32)
    cos_t = c_ref[0]
    sin_t = s_ref[0]
    lane = lax.broadcasted_iota(I32, cos_t.shape, 1)
    scale = float(MLA_QK) ** -0.5 * _LOG2E
    inv_d = 1.0 / MLA_QK
    cq = cos_t * (gq_ref[...] * scale)
    sq = sin_t * (gqs_ref[...] * scale)
    ck = cos_t * gk_ref[...]
    k_rot = krs * (sin_t * gks_ref[...])
    for h in range(MLA_HEADS):
        sl = slice(h * HEAD_PAD, (h + 1) * HEAD_PAD)
        qh = q[:, sl]
        r = lax.rsqrt(jnp.sum(qh * qh, axis=-1, keepdims=True) * inv_d + RMS_EPS)
        q_ref[0, :, sl] = ((qh * cq + qs[:, sl] * sq) * r).astype(BF16)
        kh = kn[:, sl] + kr
        r = lax.rsqrt(jnp.sum(kh * kh, axis=-1, keepdims=True) * inv_d + RMS_EPS)
        kf = (kh * ck + k_rot) * r
        kt_ref[0, h] = kf.T.astype(BF16)
        v_ref[0, h] = jnp.where(lane == MLA_V, 1.0, vv[:, sl]).astype(BF16)


def _mla_prep(um3, cos_t, sin_t, p):
    b, t, _ = um3.shape
    tt = min(512, t)
    full = lambda shape: pl.BlockSpec(shape, lambda bi, i: (0,) * len(shape))
    tab = pl.BlockSpec((1, tt, LANES), lambda bi, i: (bi, i, 0))
    return pl.pallas_call(
        _mla_prep_kernel,
        grid=(b, t // tt),
        in_specs=[pl.BlockSpec((1, tt, MLA_PAD_IN), lambda bi, i: (bi, i, 0)), tab, tab,
                  full((1, MLA_Q_LORA)), full((MLA_Q_LORA, MLA_HEADS * HEAD_PAD)),
                  full((MLA_Q_LORA, MLA_HEADS * HEAD_PAD)),
                  full((1, MLA_KV_LORA)), full((MLA_KV_LORA, MLA_HEADS * HEAD_PAD)),
                  full((MLA_KV_LORA, MLA_HEADS * HEAD_PAD)),
                  full((1, LANES)), full((1, LANES)), full((1, LANES)), full((1, LANES))],
        out_specs=[pl.BlockSpec((1, tt, MLA_HEADS * HEAD_PAD), lambda bi, i: (bi, i, 0)),
                   pl.BlockSpec((1, MLA_HEADS, HEAD_PAD, tt), lambda bi, i: (bi, 0, 0, i)),
                   pl.BlockSpec((1, MLA_HEADS, tt, HEAD_PAD), lambda bi, i: (bi, 0, i, 0))],
        out_shape=[jax.ShapeDtypeStruct((b, t, MLA_HEADS * HEAD_PAD), BF16),
                   jax.ShapeDtypeStruct((b, MLA_HEADS, HEAD_PAD, t), BF16),
                   jax.ShapeDtypeStruct((b, MLA_HEADS, t, HEAD_PAD), BF16)],
        compiler_params=_cp(2),
        name="mla_prep",
    )(um3, cos_t, sin_t, p["gql"], p["wq"], p["wqs"], p["gkvl"], p["wk"], p["wv"],
      p["gq"], p["gqs"], p["gk"], p["gks"])


_ATTN_HEAD_GROUP = 4
_LOG2E = 1.4426950408889634


def _attn_kernel(q_ref, kt_ref, v_ref, o_ref, *, tq, tk):
    i = pl.program_id(1)
    nfull = i * (tq // tk)
    row = lax.broadcasted_iota(I32, (tq, tk), 0)
    col = lax.broadcasted_iota(I32, (tq, tk), 1)
    outs = []

    def step(h, off, m, acc, mask):
        q = q_ref[0, :, h * HEAD_PAD:(h + 1) * HEAD_PAD]
        s = jnp.dot(q, kt_ref[0, h, :, pl.ds(off, tk)], preferred_element_type=F32)
        if mask is not None:
            s = jnp.where(mask, s, _NEG)
        m_new = jnp.maximum(m, jnp.max(s, axis=-1, keepdims=True))
        p = jnp.exp2(s - m_new).astype(BF16)
        acc = acc * jnp.exp2(m - m_new) + jnp.dot(p, v_ref[0, h, pl.ds(off, tk), :], preferred_element_type=F32)
        return m_new, acc

    for h0 in range(0, MLA_HEADS, _ATTN_HEAD_GROUP):
        heads = range(h0, h0 + _ATTN_HEAD_GROUP)

        def body(kb, carry):
            off = pl.multiple_of(kb * tk, tk)
            return tuple(step(h, off, carry[n][0], carry[n][1], None) for n, h in enumerate(heads))

        init = tuple((jnp.full((tq, 1), _NEG, F32), jnp.zeros((tq, HEAD_PAD), F32)) for _ in heads)
        carry = lax.fori_loop(0, nfull, body, init)
        off = pl.multiple_of(i * tq, tq)
        carry = tuple(step(h, off, carry[n][0], carry[n][1], col <= row) for n, h in enumerate(heads))
        for m, acc in carry:
            outs.append(acc[:, :MLA_V] / acc[:, MLA_V:MLA_V + 1])
    o_ref[0] = jnp.concatenate(outs, axis=-1)


def _attention(q, kt, v):
    b, t, _ = q.shape
    tq = min(512, t)
    tk = tq
    return pl.pallas_call(
        functools.partial(_attn_kernel, tq=tq, tk=tk),
        grid=(b, t // tq),
        in_specs=[pl.BlockSpec((1, tq, MLA_HEADS * HEAD_PAD), lambda bi, i: (bi, i, 0)),
                  pl.BlockSpec((1, MLA_HEADS, HEAD_PAD, t), lambda bi, i: (bi, 0, 0, 0)),
                  pl.BlockSpec((1, MLA_HEADS, t, HEAD_PAD), lambda bi, i: (bi, 0, 0, 0))],
        out_specs=pl.BlockSpec((1, tq, MLA_OUT), lambda bi, i: (bi, i, 0)),
        out_shape=jax.ShapeDtypeStruct((b, t, MLA_OUT), F32),
        compiler_params=_cp(2),
        name="mla_attention",
    )(q, kt, v)


def _rwkv_streams(u_ref, up_ref, mu_ref, w0_ref, wup_ref, a0_ref, aup_ref, gup_ref, kk_ref, ka_ref,
                  rk_ref, ones_ref):
    i = pl.program_id(1)
    u = u_ref[0]
    prow = jnp.where(i > 0, up_ref[0][7:8, :], 0.0)
    rowi = lax.broadcasted_iota(I32, u.shape, 0)
    prev = jnp.where(rowi == 0, prow, pltpu.roll(u, 1, axis=0))
    uf = u + (prev - u) * mu_ref[...]
    c = RWKV_CH
    r = uf[:, 0:c]
    k = uf[:, c:2 * c]
    v = uf[:, 2 * c:3 * c]
    wa = uf[:, 3 * c:3 * c + LANES]
    gd = uf[:, 3 * c + LANES:]
    ones_bd = ones_ref[...]
    wl = w0_ref[...] + _mm(jnp.tanh(wa), wup_ref[...])
    z = -wl
    softplus = jnp.maximum(z, 0.0) + jnp.log(1.0 + jnp.exp(-jnp.abs(z)))
    lw = -jnp.exp(-softplus - 0.5)
    a_s = _sigmoid(a0_ref[...] + _mm(wa, aup_ref[...]))
    g = _mm(_sigmoid(gd), gup_ref[...])
    kk = k * kk_ref[...]
    nrm = jnp.maximum(jnp.sqrt(_seg_sum(kk * kk, ones_bd)), 1e-12)
    kkn = kk / nrm
    k2 = k * (1.0 + (a_s - 1.0) * ka_ref[...])
    bonus = _seg_sum(r * k2 * rk_ref[...], ones_bd) * v
    return r, lw, k2, v, -kkn, kkn * a_s, bonus, g


def _intra_chunks(rs, lws, ks, vs, as_, bs, mask_bd, eye_bd, tril, strict, incl, eye_p):
    c = CHUNK
    each = lambda fn, *lists: [fn(*xs) for xs in zip(*lists)]

    def bd(x):
        xb = x.astype(BF16)
        return jnp.where(mask_bd, jnp.concatenate([xb, xb, xb, xb], axis=0), jnp.zeros((), BF16))

    def prefix(lw):
        hi = lw.astype(BF16)
        rem = lw - hi.astype(F32)
        mid = rem.astype(BF16)
        lo = (rem - mid.astype(F32)).astype(BF16)
        return (jnp.dot(tril, hi, preferred_element_type=F32) + jnp.dot(tril, mid, preferred_element_type=F32)
                + jnp.dot(tril, lo, preferred_element_type=F32))

    cums = each(prefix, lws)
    tots = each(lambda cum: cum[c - 1:c, :], cums)
    e_negs = each(lambda cum: jnp.exp(-cum), cums)
    e_ends = each(lambda tot, cum: jnp.exp(tot - cum), tots, cums)
    rts = each(lambda r, cum: r * jnp.exp(cum), rs, cums)
    ats = each(lambda a, cum, lw: a * jnp.exp(cum - lw), as_, cums, lws)
    stacks = each(lambda at, rt: jnp.concatenate([at, rt], axis=0), ats, rts)
    s1s = each(lambda st, b, e: _mm_nt(st, bd(b * e)), stacks, bs, e_negs)
    s2s = each(lambda st, k, e: _mm_nt(st, bd(k * e)), stacks, ks, e_negs)
    a_abs = each(lambda s1: jnp.where(strict, s1[:c], 0.0), s1s)
    m_rbs = each(lambda s1: jnp.where(incl, s1[c:], 0.0), s1s)
    a_aks = each(lambda s2: jnp.where(strict, s2[:c], 0.0), s2s)
    m_rks = each(lambda s2: jnp.where(incl, s2[c:], 0.0), s2s)
    xs = each(lambda a_ab: eye_p + a_ab, a_abs)
    aks = each(lambda a_ab: _mm(a_ab, bd(a_ab)), a_abs)
    for j in range(5):
        if j < 4:
            pps = each(lambda ak, x: _mm(jnp.concatenate([ak, x], axis=0), bd(ak)), aks, xs)
            xs = each(lambda x, pp: x + pp[c:], xs, pps)
            aks = each(lambda pp: pp[:c], pps)
        else:
            xs = each(lambda x, ak: x + _mm(x, bd(ak)), xs, aks)
    bdvs = each(bd, vs)
    avs = each(_mm, a_aks, bdvs)
    ahs = each(lambda x, at: _mm(x, bd(at)), xs, ats)
    uhs = each(lambda x, av: _mm(x, bd(av)), xs, avs)
    qs = each(lambda rt, m_rb, ah: rt + _mm(m_rb, bd(ah)), rts, m_rbs, ahs)
    y0s = each(lambda m_rb, uh, m_rk, bdv: _mm(m_rb, bd(uh)) + _mm(m_rk, bdv), m_rbs, uhs, m_rks, bdvs)
    gs = each(lambda b, e, ah, tot: jnp.where(mask_bd, _mm_tn(b * e, ah), 0.0) + jnp.where(eye_bd, jnp.exp(tot), 0.0),
              bs, e_ends, ahs, tots)
    fs = each(lambda b, k, e, uh, v: jnp.where(mask_bd, _mm_tn(b * e, uh) + _mm_tn(k * e, v), 0.0),
              bs, ks, e_ends, uhs, vs)
    return qs, y0s, gs, fs


def _rwkv_intra_kernel(*refs, cps):
    q_o, y0_o, g_o, f_o, bonus_o, gate_o = refs[-6:]
    r, lw, k, v, a, b, bonus, gate = _rwkv_streams(*refs[:-6])
    bonus_o[0] = bonus
    gate_o[0] = gate
    c = CHUNK
    n = RWKV_CH
    ri = lax.broadcasted_iota(I32, (n, n), 0)
    ci = lax.broadcasted_iota(I32, (n, n), 1)
    mask_bd = (ri // RWKV_HEAD) == (ci // RWKV_HEAD)
    eye_bd = ri == ci
    t64 = lax.broadcasted_iota(I32, (c, c), 0)
    i64 = lax.broadcasted_iota(I32, (c, c), 1)
    tril = jnp.where(i64 <= t64, 1.0, 0.0).astype(BF16)
    tp = lax.broadcasted_iota(I32, (c, n), 0)
    ip = lax.broadcasted_iota(I32, (c, n), 1) & (RWKV_HEAD - 1)
    strict = ip < tp
    incl = ip <= tp
    eye_p = jnp.where(ip == tp, 1.0, 0.0)
    sls = [slice(s * c, (s + 1) * c) for s in range(cps)]
    split = lambda val: [val[sl] for sl in sls]
    qs, y0s, gs, fs = _intra_chunks(split(r), split(lw), split(k), split(v), split(a), split(b),
                                    mask_bd, eye_bd, tril, strict, incl, eye_p)
    for s, sl in enumerate(sls):
        q_o[0, sl, :] = qs[s]
        y0_o[0, sl, :] = y0s[s]
        g_o[0, s] = gs[s]
        f_o[0, s] = fs[s]


def _rwkv_intra(u3, p):
    b, t, _ = u3.shape
    n = RWKV_CH
    nc = t // CHUNK
    cps = max(d for d in (1, 2, 4, 8) if nc % d == 0)
    tt = cps * CHUNK
    hb = tt // _SUBLANES
    full = lambda shape: pl.BlockSpec(shape, lambda bi, i: (0,) * len(shape))
    tok = pl.BlockSpec((1, tt, n), lambda bi, i: (bi, i, 0))
    mat = pl.BlockSpec((1, cps, n, n), lambda bi, i: (bi, i, 0, 0))
    tok_shape = jax.ShapeDtypeStruct((b, t, n), F32)
    mat_shape = jax.ShapeDtypeStruct((b, nc, n, n), F32)
    return pl.pallas_call(
        functools.partial(_rwkv_intra_kernel, cps=cps),
        grid=(b, nc // cps),
        in_specs=[pl.BlockSpec((1, tt, RWKV_IN), lambda bi, i: (bi, i, 0)),
                  pl.BlockSpec((1, _SUBLANES, RWKV_IN), lambda bi, i: (bi, jnp.maximum(i * hb - 1, 0), 0)),
                  full((1, RWKV_IN)), full((1, RWKV_CH)), full((LANES, RWKV_CH)), full((1, RWKV_CH)),
                  full((LANES, RWKV_CH)), full((LANES, RWKV_CH)), full((1, RWKV_CH)), full((1, RWKV_CH)),
                  full((1, RWKV_CH)), full((RWKV_CH, RWKV_CH))],
        out_specs=[tok, tok, mat, mat, tok, tok],
        out_shape=[tok_shape, tok_shape, mat_shape, mat_shape, tok_shape, tok_shape],
        compiler_params=_cp(2),
        name="rwkv_intra",
    )(u3, u3, p["mu"], p["w0"], p["wup"], p["a0"], p["aup"], p["gup"], p["kk"], p["ka"], p["rk"], p["ones"])


def _rwkv_seq_kernel(q_ref, y0_ref, g_ref, f_ref, o_ref, h_ref, *, nb, cs):
    c = pl.program_id(0)

    @pl.when(c == 0)
    def _():
        h_ref[...] = jnp.zeros_like(h_ref)

    for s in range(cs):
        rows = slice(s * CHUNK, (s + 1) * CHUNK)
        for b in range(nb):
            hb = h_ref[b].astype(BF16)
            o_ref[b, rows, :] = (jnp.dot(q_ref[b, rows, :].astype(BF16), hb, preferred_element_type=F32)
                                 + y0_ref[b, rows, :])
            h_ref[b] = jnp.dot(g_ref[b, s].astype(BF16), hb, preferred_element_type=F32) + f_ref[b, s]


def _rwkv_seq(q, y0, g, f):
    b, t, n = q.shape
    nc = t // CHUNK
    cs = max(d for d in (1, 2, 4) if nc % d == 0)
    tok = pl.BlockSpec((b, cs * CHUNK, n), lambda c: (0, c, 0))
    mat = pl.BlockSpec((b, cs, n, n), lambda c: (0, c, 0, 0))
    return pl.pallas_call(
        functools.partial(_rwkv_seq_kernel, nb=b, cs=cs),
        grid=(nc // cs,),
        in_specs=[tok, tok, mat, mat],
        out_specs=tok,
        out_shape=jax.ShapeDtypeStruct((b, t, n), F32),
        scratch_shapes=[pltpu.VMEM((b, n, n), F32)],
        compiler_params=_cp(1),
        name="rwkv_seq",
    )(q, y0, g, f)


def _store_token_tiles(ref, val, tm):
    for p in range(D_MODEL // LANES):
        ref[pl.ds(p, tm, stride=_ROW_TILE), :] = val[:, p * LANES:(p + 1) * LANES]


def _load_token_tiles(ref, tm):
    return jnp.concatenate([ref[pl.ds(p, tm, stride=_ROW_TILE), :] for p in range(D_MODEL // LANES)], axis=-1)


def _out_proj_kernel(x_ref, yc_ref, ym_ref, yw_ref, bonus_ref, gate_ref, lng_ref, lnb_ref, ones_ref,
                     w_ref, gt_ref, g2_ref, sc_ref, sh_ref, *rest, with_router, tm):
    if with_router:
        wr_ref, xo_ref, h_ref, lg_ref = rest
    else:
        xo_ref, h_ref = rest
    ones_bd = ones_ref[...]
    inv = 1.0 / RWKV_HEAD
    y = yw_ref[...]
    mu = _seg_sum(y, ones_bd) * inv
    yc = y - mu
    var = _seg_sum(yc * yc, ones_bd) * inv
    yn = yc * lax.rsqrt(var + RWKV_GN_EPS) * lng_ref[...] + lnb_ref[...]
    yr = (yn + bonus_ref[...]) * gate_ref[...]
    mixed = (_mm(yc_ref[...], w_ref[0:CONV_CH, :])
             + _mm(ym_ref[...], w_ref[CONV_CH:CONV_CH + MLA_OUT, :])
             + _mm(yr, w_ref[CONV_CH + MLA_OUT:, :]))
    x = x_ref[...] + gt_ref[0] * mixed
    xo_ref[...] = x
    ms = jnp.mean(x * x, axis=-1, keepdims=True)
    h = x * lax.rsqrt(ms + RMS_EPS) * g2_ref[...]
    h = h * (1.0 + sc_ref[0]) + sh_ref[0]
    if with_router:
        _store_token_tiles(h_ref, h, tm)
    else:
        h_ref[...] = h
    if with_router:
        h_hi, h_lo = _split2(h)
        w_hi, w_lo = _split2(wr_ref[...])
        lg_ref[...] = (jnp.dot(h_hi, w_hi, preferred_element_type=F32)
                       + jnp.dot(h_lo, w_hi, preferred_element_type=F32)
                       + jnp.dot(h_hi, w_lo, preferred_element_type=F32))


def _out_proj(x2, yc, ym, yw, bonus, gate, ln_g, ln_b, ones_bd, w_out, gt, g2, sc, sh, router, t):
    n = x2.shape[0]
    tm = min(512, t)
    tpb = t // tm
    vec = pl.BlockSpec((1, 1, D_MODEL), lambda i: (i // tpb, 0, 0))
    row = lambda w: pl.BlockSpec((tm, w), lambda i: (i, 0))
    full = lambda r, c: pl.BlockSpec((r, c), lambda i: (0, 0))
    in_specs = [row(D_MODEL), row(CONV_CH), row(MLA_OUT), row(RWKV_CH), row(RWKV_CH), row(RWKV_CH),
                full(1, RWKV_CH), full(1, RWKV_CH), full(RWKV_CH, RWKV_CH),
                full(D_MODEL, D_MODEL), vec, full(1, D_MODEL), vec, vec]
    out_specs = [row(D_MODEL), row(D_MODEL)]
    out_shape = [jax.ShapeDtypeStruct((n, D_MODEL), F32), jax.ShapeDtypeStruct((n, D_MODEL), F32)]
    args = [x2, yc, ym, yw, bonus, gate, ln_g, ln_b, ones_bd, w_out, gt, g2, sc, sh]
    if router is not None:
        in_specs.append(full(D_MODEL, LANES))
        out_specs = [row(D_MODEL), pl.BlockSpec((tm * _ROW_TILE, LANES), lambda i: (i, 0)), row(LANES)]
        out_shape = [out_shape[0], jax.ShapeDtypeStruct((n * _ROW_TILE, LANES), F32),
                     jax.ShapeDtypeStruct((n, LANES), F32)]
        args.append(router)
    return pl.pallas_call(
        functools.partial(_out_proj_kernel, with_router=router is not None, tm=tm),
        grid=(n // tm,),
        in_specs=in_specs, out_specs=out_specs, out_shape=out_shape,
        compiler_params=_cp(1),
        name="out_proj",
    )(*args)


def _swiglu_step(load_h, wg_ref, wu_ref, wd_ref, hb_ref, acc_ref, j):
    @pl.when(j == 0)
    def _():
        hb_ref[...] = load_h().astype(BF16)
        acc_ref[...] = jnp.zeros_like(acc_ref)

    hb = hb_ref[...]
    gate = jnp.dot(hb, wg_ref[...], preferred_element_type=F32)
    up = jnp.dot(hb, wu_ref[...], preferred_element_type=F32)
    act = (gate * _sigmoid(gate) * up).astype(BF16)
    acc_ref[...] += jnp.dot(act, wd_ref[...], preferred_element_type=F32)


def _ffn_kernel(h_ref, x_ref, gt_ref, wg_ref, wu_ref, wd_ref, o_ref, hb_ref, acc_ref, *, nj):
    j = pl.program_id(1)
    _swiglu_step(lambda: h_ref[...], wg_ref, wu_ref, wd_ref, hb_ref, acc_ref, j)

    @pl.when(j == nj - 1)
    def _():
        o_ref[...] = x_ref[...] + gt_ref[0] * acc_ref[...]


def _dense_ffn(h2, x2, gt, wg, wu, wd, t):
    n = x2.shape[0]
    ff = wg.shape[1]
    tm = min(512, t)
    tpb = t // tm
    nj = 2
    tf = ff // nj
    row = pl.BlockSpec((tm, D_MODEL), lambda i, j: (i, 0))
    return pl.pallas_call(
        functools.partial(_ffn_kernel, nj=nj),
        grid=(n // tm, nj),
        in_specs=[row, row, pl.BlockSpec((1, 1, D_MODEL), lambda i, j: (i // tpb, 0, 0)),
                  pl.BlockSpec((D_MODEL, tf), lambda i, j: (0, j)),
                  pl.BlockSpec((D_MODEL, tf), lambda i, j: (0, j)),
                  pl.BlockSpec((tf, D_MODEL), lambda i, j: (j, 0))],
        out_specs=row,
        out_shape=jax.ShapeDtypeStruct((n, D_MODEL), F32),
        scratch_shapes=[pltpu.VMEM((tm, D_MODEL), BF16), pltpu.VMEM((tm, D_MODEL), F32)],
        compiler_params=_cp(2),
        name="dense_ffn",
    )(h2, x2, gt, wg, wu, wd)


def _moe_ffn_kernel(te_ref, nu_ref, xs_ref, wg_ref, wu_ref, wd_ref, o_ref, hb_ref, acc_ref, *, nj, tm):
    i = pl.program_id(0)
    j = pl.program_id(1)
    used = i < nu_ref[0]

    @pl.when(used)
    def _():
        _swiglu_step(lambda: _load_token_tiles(xs_ref, tm), wg_ref.at[0], wu_ref.at[0], wd_ref.at[0],
                     hb_ref, acc_ref, j)

    @pl.when(j == nj - 1)
    def _():
        _store_token_tiles(o_ref, jnp.where(used, acc_ref[...], 0.0), tm)


def _moe_ffn(xs, tile_e, n_used, wg, wu, wd):
    cap = xs.shape[0] // _ROW_TILE
    ff = wg.shape[2]
    nj = 2
    tf = ff // nj
    tm = MOE_TILE

    def wj(i, j, nu):
        return jnp.where(i < nu[0], j, nj - 1)

    row = pl.BlockSpec((tm * _ROW_TILE, LANES), lambda i, j, te, nu: (i, 0))
    grid_spec = pltpu.PrefetchScalarGridSpec(
        num_scalar_prefetch=2,
        grid=(cap // tm, nj),
        in_specs=[row,
                  pl.BlockSpec((1, D_MODEL, tf), lambda i, j, te, nu: (te[i], 0, wj(i, j, nu))),
                  pl.BlockSpec((1, D_MODEL, tf), lambda i, j, te, nu: (te[i], 0, wj(i, j, nu))),
                  pl.BlockSpec((1, tf, D_MODEL), lambda i, j, te, nu: (te[i], wj(i, j, nu), 0))],
        out_specs=row,
        scratch_shapes=[pltpu.VMEM((tm, D_MODEL), BF16), pltpu.VMEM((tm, D_MODEL), F32)])
    return pl.pallas_call(
        functools.partial(_moe_ffn_kernel, nj=nj, tm=tm),
        grid_spec=grid_spec,
        out_shape=jax.ShapeDtypeStruct((cap * _ROW_TILE, LANES), F32),
        compiler_params=_cp(2),
        name="moe_ffn",
    )(tile_e, n_used, xs, wg, wu, wd)


def _route_kernel(lg_ref, oi_ref, of_ref, cnt_ref, carry_ref, *, tm):
    i = pl.program_id(0)

    @pl.when(i == 0)
    def _():
        carry_ref[...] = jnp.zeros_like(carry_ref)

    lane = lax.broadcasted_iota(I32, (tm, LANES), 1)
    real = lane < N_EXPERTS
    lg = jnp.where(real, lg_ref[...], _NEG)
    ex = jnp.exp(lg - jnp.max(lg, axis=-1, keepdims=True))
    p = ex / jnp.sum(ex, axis=-1, keepdims=True)
    p = jnp.where(real, p, -1.0)
    m1 = jnp.max(p, axis=-1, keepdims=True)
    i1 = jnp.min(jnp.where(p == m1, lane, LANES), axis=-1, keepdims=True)
    p2 = jnp.where(lane == i1, -1.0, p)
    m2 = jnp.max(p2, axis=-1, keepdims=True)
    i2 = jnp.min(jnp.where(p2 == m2, lane, LANES), axis=-1, keepdims=True)
    den = m1 + m2
    oh1 = lane == i1
    oh2 = lane == i2
    oh = jnp.where(oh1 | oh2, 1.0, 0.0)
    rr = lax.broadcasted_iota(I32, (tm, tm), 0)
    cc = lax.broadcasted_iota(I32, (tm, tm), 1)
    lstrict = jnp.where(cc < rr, 1.0, 0.0).astype(BF16)
    pre = jnp.dot(lstrict, oh.astype(BF16), preferred_element_type=F32) + carry_ref[...]
    rank1 = jnp.sum(jnp.where(oh1, pre, 0.0), axis=-1, keepdims=True).astype(I32)
    rank2 = jnp.sum(jnp.where(oh2, pre, 0.0), axis=-1, keepdims=True).astype(I32)
    carry_ref[...] = carry_ref[...] + jnp.sum(oh, axis=0, keepdims=True)
    cnt_ref[...] = carry_ref[...]
    oi_ref[...] = jnp.where(lane == 0, i1, jnp.where(lane == 1, i2, jnp.where(lane == 2, rank1,
                                                                                jnp.where(lane == 3, rank2, 0))))
    of_ref[...] = jnp.where(lane == 0, m1 / den, jnp.where(lane == 1, m2 / den, 0.0))


def _route(logits, t):
    n = logits.shape[0]
    tm = min(512, t)
    row = pl.BlockSpec((tm, LANES), lambda i: (i, 0))
    return pl.pallas_call(
        functools.partial(_route_kernel, tm=tm),
        grid=(n // tm,),
        in_specs=[row],
        out_specs=[row, row, pl.BlockSpec((1, LANES), lambda i: (0, 0))],
        out_shape=[jax.ShapeDtypeStruct((n, LANES), I32), jax.ShapeDtypeStruct((n, LANES), F32),
                   jax.ShapeDtypeStruct((1, LANES), F32)],
        scratch_shapes=[pltpu.VMEM((1, LANES), F32)],
        compiler_params=_cp(1),
        name="moe_route",
    )(logits)


def _dispatch_kernel(s0_ref, s1_ref, h_ref, init_ref, xs_ref, sem, *, tm):
    del init_ref
    base = pl.program_id(0) * tm

    def row_copy(r, slot):
        return pltpu.make_async_copy(h_ref.at[pl.ds(pl.multiple_of(r * _ROW_TILE, _ROW_TILE), _ROW_TILE)],
                                     xs_ref.at[pl.ds(pl.multiple_of(slot * _ROW_TILE, _ROW_TILE), _ROW_TILE)],
                                     sem)

    def start(r, carry):
        row_copy(r, s0_ref[base + r]).start(priority=0)
        row_copy(r, s1_ref[base + r]).start(priority=1)
        return carry

    lax.fori_loop(0, tm, start, 0)

    def wait(r, carry):
        row_copy(r, s0_ref[base + r]).wait()
        row_copy(r, s1_ref[base + r]).wait()
        return carry

    lax.fori_loop(0, tm, wait, 0)


def _dispatch(h2, slot0, slot1, cap, t):
    n = h2.shape[0] // _ROW_TILE
    tm = min(512, t)
    grid_spec = pltpu.PrefetchScalarGridSpec(
        num_scalar_prefetch=2,
        grid=(n // tm,),
        in_specs=[pl.BlockSpec((tm * _ROW_TILE, LANES), lambda i, s0, s1: (i, 0)),
                  pl.BlockSpec(memory_space=pl.ANY)],
        out_specs=pl.BlockSpec(memory_space=pl.ANY),
        scratch_shapes=[pltpu.SemaphoreType.DMA(())])
    return pl.pallas_call(
        functools.partial(_dispatch_kernel, tm=tm),
        grid_spec=grid_spec,
        out_shape=jax.ShapeDtypeStruct((cap * _ROW_TILE, LANES), F32),
        input_output_aliases={3: 0},
        compiler_params=_cp(1),
        name="moe_dispatch",
    )(slot0, slot1, h2, jnp.zeros((cap * _ROW_TILE, LANES), F32))


def _combine_kernel(s0_ref, s1_ref, x_ref, wf_ref, gt_ref, ys_ref, o_ref, buf0, buf1, sem, *, tm):
    base = pl.program_id(0) * tm

    def row_copy(slot, buf, r):
        return pltpu.make_async_copy(ys_ref.at[pl.ds(pl.multiple_of(slot * _ROW_TILE, _ROW_TILE), _ROW_TILE)],
                                     buf.at[pl.ds(pl.multiple_of(r * _ROW_TILE, _ROW_TILE), _ROW_TILE)], sem)

    def start(r, carry):
        row_copy(s0_ref[base + r], buf0, r).start(priority=0)
        row_copy(s1_ref[base + r], buf1, r).start(priority=1)
        return carry

    lax.fori_loop(0, tm, start, 0)

    def wait(r, carry):
        row_copy(s0_ref[base + r], buf0, r).wait()
        row_copy(s1_ref[base + r], buf1, r).wait()
        return carry

    lax.fori_loop(0, tm, wait, 0)
    wf = wf_ref[...]
    f = wf[:, 0:1] * _load_token_tiles(buf0, tm) + wf[:, 1:2] * _load_token_tiles(buf1, tm)
    o_ref[...] = x_ref[...] + gt_ref[0] * f


def _combine(x2, ys, slot0, slot1, wf, gt, t):
    n = x2.shape[0]
    tm = min(512, t)
    tpb = t // tm
    row = pl.BlockSpec((tm, D_MODEL), lambda i, s0, s1: (i, 0))
    grid_spec = pltpu.PrefetchScalarGridSpec(
        num_scalar_prefetch=2,
        grid=(n // tm,),
        in_specs=[row, pl.BlockSpec((tm, LANES), lambda i, s0, s1: (i, 0)),
                  pl.BlockSpec((1, 1, D_MODEL), lambda i, s0, s1: (i // tpb, 0, 0)),
                  pl.BlockSpec(memory_space=pl.ANY)],
        out_specs=row,
        scratch_shapes=[pltpu.VMEM((tm * _ROW_TILE, LANES), F32), pltpu.VMEM((tm * _ROW_TILE, LANES), F32),
                        pltpu.SemaphoreType.DMA(())])
    return pl.pallas_call(
        functools.partial(_combine_kernel, tm=tm),
        grid_spec=grid_spec,
        out_shape=jax.ShapeDtypeStruct((n, D_MODEL), F32),
        compiler_params=_cp(1),
        name="moe_combine",
    )(slot0, slot1, x2, wf, gt, ys)


def _moe(h2, x2, logits, gt, wg, wu, wd, t):
    n = x2.shape[0]
    info_i, info_f, cnt = _route(logits, t)
    counts = cnt[0, :N_EXPERTS].astype(I32)
    padded = (counts + MOE_TILE - 1) // MOE_TILE * MOE_TILE
    pad_ends = jnp.cumsum(padded)
    pad_starts = pad_ends - padded
    slot0 = pad_starts[info_i[:, 0]] + info_i[:, 2]
    slot1 = pad_starts[info_i[:, 1]] + info_i[:, 3]
    cap = (2 * n // MOE_TILE + N_EXPERTS) * MOE_TILE
    n_tiles = cap // MOE_TILE
    n_used = (pad_ends[-1:] // MOE_TILE).astype(I32)
    tile_e = jnp.minimum(jnp.searchsorted(pad_ends, jnp.arange(n_tiles, dtype=I32) * MOE_TILE, side="right"),
                         N_EXPERTS - 1).astype(I32)
    last_e = tile_e[jnp.maximum(n_used[0] - 1, 0)]
    tile_e = jnp.where(jnp.arange(n_tiles) < n_used[0], tile_e, last_e)
    xs = _dispatch(h2, slot0, slot1, cap, t)
    ys = _moe_ffn(xs, tile_e, n_used, wg, wu, wd)
    return _combine(x2, ys, slot0, slot1, info_f, gt, t)


def _cast_kernel(a_ref, b_ref, c_ref, ao_ref, bo_ref, co_ref):
    ao_ref[...] = a_ref[...].astype(BF16)
    bo_ref[...] = b_ref[...].astype(BF16)
    co_ref[...] = c_ref[...].astype(BF16)


def _expert_weights_bf16(layer, w_gate, w_up, w_down):
    _, ne, d, ff = w_gate.shape
    rows = 256
    gu = pl.BlockSpec((None, 1, rows, ff), lambda e, r: (layer, e, r, 0))
    dn = pl.BlockSpec((None, 1, rows * ff // d, d), lambda e, r: (layer, e, r, 0))
    gu_o = pl.BlockSpec((1, rows, ff), lambda e, r: (e, r, 0))
    dn_o = pl.BlockSpec((1, rows * ff // d, d), lambda e, r: (e, r, 0))
    return pl.pallas_call(
        _cast_kernel,
        grid=(ne, d // rows),
        in_specs=[gu, gu, dn],
        out_specs=[gu_o, gu_o, dn_o],
        out_shape=[jax.ShapeDtypeStruct((ne, d, ff), BF16), jax.ShapeDtypeStruct((ne, d, ff), BF16),
                   jax.ShapeDtypeStruct((ne, ff, d), BF16)],
        compiler_params=_cp(2),
        name="expert_weights_bf16",
    )(w_gate, w_up, w_down)


def _pad_last(a, width):
    return jnp.pad(a, [(0, 0)] * (a.ndim - 1) + [(0, width - a.shape[-1])])


def _swap_rope(a):
    h = MLA_ROPE // 2
    return jnp.concatenate([a[..., :MLA_NOPE], a[..., MLA_NOPE + h:], a[..., MLA_NOPE:MLA_NOPE + h]], axis=-1)


def _layer_params(l, w_in, mla_q_norm_g, mla_w_uq, mla_kv_norm_g, mla_w_ukv, qk_norm_q, qk_norm_k,
                  rwkv_mu, rwkv_w0, rwkv_w_up, rwkv_a0, rwkv_a_up, rwkv_g_up, rwkv_k_k, rwkv_k_a, rwkv_r_k):
    w = w_in[l]
    m0 = CONV_IN
    kr0 = m0 + MLA_Q_LORA + MLA_KV_LORA
    kr = w[:, kr0:kr0 + MLA_ROPE]
    h = MLA_ROPE // 2
    zeros = lambda c: jnp.zeros((D_MODEL, c), F32)
    kr_tile = jnp.concatenate([zeros(MLA_NOPE), kr, zeros(LANES - MLA_QK)], axis=1)
    krs_tile = jnp.concatenate([zeros(MLA_NOPE), kr[:, h:], kr[:, :h], zeros(LANES - MLA_QK)], axis=1)
    w_in_p = jnp.concatenate([w[:, :kr0], kr_tile, krs_tile, w[:, kr0 + MLA_ROPE:]], axis=1).astype(BF16)

    wq = mla_w_uq[l].reshape(MLA_Q_LORA, MLA_HEADS, MLA_QK)
    wkv = mla_w_ukv[l].reshape(MLA_KV_LORA, MLA_HEADS, MLA_NOPE + MLA_V)
    flat = lambda a: _pad_last(a, HEAD_PAD).reshape(a.shape[0], MLA_HEADS * HEAD_PAD).astype(BF16)
    mla = dict(
        gql=mla_q_norm_g[l].reshape(1, -1), gkvl=mla_kv_norm_g[l].reshape(1, -1),
        wq=flat(wq), wqs=flat(_swap_rope(wq)),
        wk=flat(wkv[..., :MLA_NOPE]), wv=flat(wkv[..., MLA_NOPE:]),
        gq=_pad_last(qk_norm_q[l], LANES).reshape(1, LANES),
        gqs=_pad_last(_swap_rope(qk_norm_q[l]), LANES).reshape(1, LANES),
        gk=_pad_last(qk_norm_k[l], LANES).reshape(1, LANES),
        gks=_pad_last(_swap_rope(qk_norm_k[l]), LANES).reshape(1, LANES))

    zl = jnp.zeros((LANES // 2, RWKV_CH), F32)
    hid = jnp.arange(RWKV_CH) // RWKV_HEAD
    rwkv = dict(
        mu=rwkv_mu[l].reshape(1, -1), w0=rwkv_w0[l].reshape(1, -1), a0=rwkv_a0[l].reshape(1, -1),
        wup=jnp.concatenate([rwkv_w_up[l], zl], axis=0).astype(BF16),
        aup=jnp.concatenate([zl, rwkv_a_up[l]], axis=0).astype(BF16),
        gup=rwkv_g_up[l].astype(BF16),
        kk=rwkv_k_k[l].reshape(1, -1), ka=rwkv_k_a[l].reshape(1, -1), rk=rwkv_r_k[l].reshape(1, -1),
        ones=(hid[:, None] == hid[None, :]).astype(BF16))
    return w_in_p, mla, rwkv


def kernel(x, c, positions, ada_w, ada_b, ada_layer_bias, norm1_g, norm2_g, w_in, w_out, conv_w, conv_b, conv_ln_g, conv_ln_b, mla_q_norm_g, mla_w_uq, mla_kv_norm_g, mla_w_ukv, qk_norm_q, qk_norm_k, rwkv_mu, rwkv_w0, rwkv_w_up, rwkv_a0, rwkv_a_up, rwkv_g_up, rwkv_k_k, rwkv_k_a, rwkv_r_k, rwkv_ln_g, rwkv_ln_b, ffn_w_gate, ffn_w_up, ffn_w_down, moe_router, moe_w_gate, moe_w_up, moe_w_down):
    b, t, d = x.shape
    n = b * t
    mods = _cond(c, ada_w, ada_b, ada_layer_bias)
    cos_t, sin_t = _rope_tables(positions)
    x2 = x.reshape(n, d)
    for l in range(DEPTH):
        sh1, sc1, gt1, sh2, sc2, gt2 = [mods[l, :, k * d:(k + 1) * d].reshape(b, 1, d) for k in range(N_MOD)]
        w_in_p, mla_p, rwkv_p = _layer_params(
            l, w_in, mla_q_norm_g, mla_w_uq, mla_kv_norm_g, mla_w_ukv, qk_norm_q, qk_norm_k,
            rwkv_mu, rwkv_w0, rwkv_w_up, rwkv_a0, rwkv_a_up, rwkv_g_up, rwkv_k_k, rwkv_k_a, rwkv_r_k)
        u_conv, u_mla, u_rwkv = _in_proj(x2, sc1, sh1, norm1_g[l].reshape(1, d), w_in_p, t)
        y_conv = _conformer_conv(u_conv.reshape(b, t, CONV_IN), conv_w[l], conv_b[l], conv_ln_g[l], conv_ln_b[l])
        q, kt, v = _mla_prep(u_mla.reshape(b, t, MLA_PAD_IN), cos_t, sin_t, mla_p)
        y_mla = _attention(q, kt, v)
        qc, y0, g_, f_, bonus, gate = _rwkv_intra(u_rwkv.reshape(b, t, RWKV_IN), rwkv_p)
        y_wkv = _rwkv_seq(qc, y0, g_, f_)
        i = l // 2
        router = None
        if l % 2 == 1:
            router = _pad_last(moe_router[i], LANES)
        flat = lambda a: a.reshape(n, a.shape[-1])
        outs = _out_proj(x2, flat(y_conv), flat(y_mla), flat(y_wkv), flat(bonus), flat(gate),
                         rwkv_ln_g[l].reshape(1, -1), rwkv_ln_b[l].reshape(1, -1), rwkv_p["ones"],
                         w_out[l].astype(BF16), gt1, norm2_g[l].reshape(1, d), sc2, sh2, router, t)
        if l % 2 == 0:
            x2, h2 = outs
            x2 = _dense_ffn(h2, x2, gt2, ffn_w_gate[i].astype(BF16), ffn_w_up[i].astype(BF16),
                            ffn_w_down[i].astype(BF16), t)
        else:
            x2, h2, logits = outs
            wg_b, wu_b, wd_b = _expert_weights_bf16(i, moe_w_gate, moe_w_up, moe_w_down)
            x2 = _moe(h2, x2, logits, gt2, wg_b, wu_b, wd_b, t)
    return x2.reshape(b, t, d)
```
